```python
import jax, jax.numpy as jnp
from jax import lax
import numpy as np

D_MODEL = 4096
BATCH = 2
SEQ = 8192
DEPTH = 4

N_MIXERS = 2
N_POOL_LAYERS = (DEPTH + 1) // 2
N_GMLP_LAYERS = DEPTH // 2
POOL_WINDOWS = (2, 4, 8, 16)
N_POOL_GROUPS = len(POOL_WINDOWS)
POOL_GROUP_W = D_MODEL // N_POOL_GROUPS
GMLP_WIDTH = D_MODEL
GMLP_HEADS = 8
GMLP_HEAD_W = GMLP_WIDTH // GMLP_HEADS
GMLP_CHUNK = 128
N_EXPERTS = 32
TOP_K = 4
D_EXPERT = 384
SWIGLU_LIMIT = 7.0
SWIGLU_ALPHA = 1.702
COND_RANK = 512
N_MOD = 6
MOE_BLOCK = 128
NORM_EPS = 1e-5

kernel_name = 'hybrid_pool_gmlp_moe_adaln'


def _rms_norm(x, g):
    xf = x.astype(jnp.float32)
    y = xf * lax.rsqrt(jnp.mean(xf * xf, axis=-1, keepdims=True) + NORM_EPS)
    return (y * g.astype(jnp.float32)).astype(x.dtype)


def _layer_norm(x, g, b):
    xf = x.astype(jnp.float32)
    mu = jnp.mean(xf, axis=-1, keepdims=True)
    xc = xf - mu
    y = xc * lax.rsqrt(jnp.mean(xc * xc, axis=-1, keepdims=True) + NORM_EPS)
    return (y * g.astype(jnp.float32) + b.astype(jnp.float32)).astype(x.dtype)


def _pool_mixer(h, w_in, w_group, scale, w_out):
    bsz, s, _ = h.shape
    u = (h @ w_in).reshape(bsz, s, N_POOL_GROUPS, POOL_GROUP_W)
    uf = u.astype(jnp.float32)
    cs = jnp.cumsum(uf, axis=1)
    pos = jnp.arange(s)
    outs = []
    for g, w in enumerate(POOL_WINDOWS):
        csg = cs[:, :, g]
        lagged = jnp.pad(csg, ((0, 0), (w, 0), (0, 0)))[:, :s]
        cnt = jnp.minimum(pos + 1, w).astype(jnp.float32)[None, :, None]
        outs.append((csg - lagged) / cnt - uf[:, :, g])
    pooled = jnp.stack(outs, axis=2).astype(h.dtype)
    mixed = jnp.einsum('bsgc,gcd->bsgd', pooled, w_group).reshape(bsz, s, D_MODEL)
    return (mixed * scale) @ w_out


def _gmlp_mixer(h, w_in, b_in, ln_g, ln_b, w_s, b_s, w_out):
    bsz, s, _ = h.shape
    uv = jax.nn.gelu(h @ w_in + b_in, approximate=False)
    u, v = uv[..., :GMLP_WIDTH], uv[..., GMLP_WIDTH:]
    v = _layer_norm(v, ln_g, ln_b)
    n_chunks = s // GMLP_CHUNK
    vc = v.reshape(bsz, n_chunks, GMLP_CHUNK, GMLP_HEADS, GMLP_HEAD_W)
    causal = jnp.tril(jnp.ones((GMLP_CHUNK, GMLP_CHUNK), dtype=bool))
    w_causal = jnp.where(causal[None], w_s, jnp.zeros_like(w_s))
    mixed = jnp.einsum('hts,bnshc->bnthc', w_causal, vc)
    mixed = mixed + jnp.swapaxes(b_s, 0, 1)[None, None, :, :, None]
    return (u * mixed.reshape(bsz, s, GMLP_WIDTH)) @ w_out


def _clamped_swiglu(hu):
    glu, lin = hu[..., :D_EXPERT], hu[..., D_EXPERT:]
    glu = jnp.minimum(glu, SWIGLU_LIMIT)
    lin = jnp.clip(lin, -SWIGLU_LIMIT, SWIGLU_LIMIT)
    return glu * jax.nn.sigmoid(SWIGLU_ALPHA * glu) * (lin + 1.0)


def _moe(h, w_router, b_router, w_up, b_up, w_down, b_down):
    bsz, s, d = h.shape
    n_tok = bsz * s
    xt = h.reshape(n_tok, d)
    logits = xt.astype(jnp.float32) @ w_router.astype(jnp.float32) + b_router.astype(jnp.float32)
    top_vals, top_idx = lax.top_k(logits, TOP_K)
    gates = jax.nn.softmax(top_vals, axis=-1)
    e_flat = top_idx.reshape(-1).astype(jnp.int32)
    tok_flat = jnp.repeat(jnp.arange(n_tok, dtype=jnp.int32), TOP_K)
    g_flat = gates.reshape(-1)
    order = jnp.argsort(e_flat)
    e_sorted = e_flat[order]
    counts = jnp.bincount(e_flat, length=N_EXPERTS)
    padded = (counts + MOE_BLOCK - 1) // MOE_BLOCK * MOE_BLOCK
    start = jnp.cumsum(counts) - counts
    pend = jnp.cumsum(padded)
    pstart = pend - padded
    n_pairs = n_tok * TOP_K
    dest = pstart[e_sorted] + (jnp.arange(n_pairs, dtype=jnp.int32) - start[e_sorted])
    n_rows = n_pairs + N_EXPERTS * MOE_BLOCK
    n_blocks = n_rows // MOE_BLOCK
    row_tok = jnp.zeros((n_rows,), jnp.int32).at[dest].set(tok_flat[order])
    row_gate = jnp.zeros((n_rows,), jnp.float32).at[dest].set(g_flat[order])
    blk_start = jnp.arange(n_blocks, dtype=jnp.int32) * MOE_BLOCK
    blk_expert = jnp.minimum(jnp.searchsorted(pend, blk_start, side='right'), N_EXPERTS - 1)

    def block_fn(args):
        e, toks = args
        xb = xt[toks]
        act = _clamped_swiglu(xb @ w_up[e] + b_up[e])
        return act @ w_down[e] + b_down[e]

    yb = lax.map(block_fn, (blk_expert, row_tok.reshape(n_blocks, MOE_BLOCK)))
    yb = yb.reshape(n_rows, d) * row_gate[:, None].astype(yb.dtype)
    out = jax.ops.segment_sum(yb, row_tok, num_segments=n_tok)
    return out.reshape(bsz, s, d)


def setup_inputs(seed: int = 0) -> dict:
    key = jax.random.key(seed)
    ks = jax.random.split(key, 28)
    f32 = jnp.float32

    def nrm(k, shape, scale):
        return jax.random.normal(k, shape, f32) * scale

    D, E2 = D_MODEL, 2 * GMLP_WIDTH
    return {
        'x': nrm(ks[0], (BATCH, SEQ, D), 1.0),
        'c': nrm(ks[1], (BATCH, D), 1.0),
        'w_cond': nrm(ks[2], (D, COND_RANK), D ** -0.5),
        'b_cond': nrm(ks[3], (COND_RANK,), 0.02),
        'w_mod': nrm(ks[4], (DEPTH, COND_RANK, N_MOD * D), 0.3 * COND_RANK ** -0.5),
        'b_mod': nrm(ks[5], (DEPTH, N_MOD * D), 0.02),
        'g_norm_mix': 1.0 + nrm(ks[6], (DEPTH, D), 0.02),
        'g_norm_ffn': 1.0 + nrm(ks[7], (DEPTH, D), 0.02),
        'pool_w_in': nrm(ks[8], (N_POOL_LAYERS, D, D), D ** -0.5),
        'pool_w_group': nrm(ks[9], (N_POOL_LAYERS, N_POOL_GROUPS, POOL_GROUP_W, POOL_GROUP_W), POOL_GROUP_W ** -0.5),
        'pool_scale': 1.0 + nrm(ks[10], (N_POOL_LAYERS, D), 0.1),
        'pool_w_out': nrm(ks[11], (N_POOL_LAYERS, D, D), D ** -0.5),
        'gmlp_w_in': nrm(ks[12], (N_GMLP_LAYERS, D, E2), D ** -0.5),
        'gmlp_b_in': nrm(ks[13], (N_GMLP_LAYERS, E2), 0.02),
        'gmlp_ln_g': 1.0 + nrm(ks[14], (N_GMLP_LAYERS, GMLP_WIDTH), 0.02),
        'gmlp_ln_b': nrm(ks[15], (N_GMLP_LAYERS, GMLP_WIDTH), 0.02),
        'gmlp_w_s': nrm(ks[16], (N_GMLP_LAYERS, GMLP_HEADS, GMLP_CHUNK, GMLP_CHUNK), 0.5 * GMLP_CHUNK ** -0.5),
        'gmlp_b_s': 1.0 + nrm(ks[17], (N_GMLP_LAYERS, GMLP_HEADS, GMLP_CHUNK), 0.1),
        'gmlp_w_out': nrm(ks[18], (N_GMLP_LAYERS, GMLP_WIDTH, D), GMLP_WIDTH ** -0.5),
        'moe_w_router': nrm(ks[19], (DEPTH, D, N_EXPERTS), D ** -0.5),
        'moe_b_router': nrm(ks[20], (DEPTH, N_EXPERTS), 0.01),
        'moe_w_up': nrm(ks[21], (DEPTH, N_EXPERTS, D, 2 * D_EXPERT), D ** -0.5),
        'moe_b_up': nrm(ks[22], (DEPTH, N_EXPERTS, 2 * D_EXPERT), 0.02),
        'moe_w_down': nrm(ks[23], (DEPTH, N_EXPERTS, D_EXPERT, D), D_EXPERT ** -0.5),
        'moe_b_down': nrm(ks[24], (DEPTH, N_EXPERTS, D), 0.02),
        'g_final': 1.0 + nrm(ks[25], (D,), 0.02),
    }


def reference(x, c, w_cond, b_cond, w_mod, b_mod, g_norm_mix, g_norm_ffn,
              pool_w_in, pool_w_group, pool_scale, pool_w_out,
              gmlp_w_in, gmlp_b_in, gmlp_ln_g, gmlp_ln_b, gmlp_w_s, gmlp_b_s, gmlp_w_out,
              moe_w_router, moe_b_router, moe_w_up, moe_b_up, moe_w_down, moe_b_down,
              g_final):
    cond = jax.nn.silu(c @ w_cond + b_cond)
    for layer in range(DEPTH):
        mod = (cond @ w_mod[layer] + b_mod[layer])[:, None, :]
        sh1, sc1, gt1, sh2, sc2, gt2 = jnp.split(mod, N_MOD, axis=-1)
        j = layer // N_MIXERS
        h = _rms_norm(x, g_norm_mix[layer]) * (1.0 + sc1) + sh1
        if layer % N_MIXERS == 0:
            y = _pool_mixer(h, pool_w_in[j], pool_w_group[j], pool_scale[j], pool_w_out[j])
        else:
            y = _gmlp_mixer(h, gmlp_w_in[j], gmlp_b_in[j], gmlp_ln_g[j], gmlp_ln_b[j],
                            gmlp_w_s[j], gmlp_b_s[j], gmlp_w_out[j])
        x = x + gt1 * y
        h = _rms_norm(x, g_norm_ffn[layer]) * (1.0 + sc2) + sh2
        x = x + gt2 * _moe(h, moe_w_router[layer], moe_b_router[layer], moe_w_up[layer],
                           moe_b_up[layer], moe_w_down[layer], moe_b_down[layer])
    return _rms_norm(x, g_final)
```

```python
import functools

import jax
import jax.numpy as jnp
from jax import lax
from jax.experimental import pallas as pl
from jax.experimental.pallas import tpu as pltpu

F32 = jnp.float32
BF16 = jnp.bfloat16
U32 = jnp.uint32
I32 = jnp.int32
HIGHEST = lax.Precision.HIGHEST

NORM_EPS = 1e-5
POOL_WINDOWS = (2, 4, 8, 16)
TOP_K = 4
SWIGLU_LIMIT = 7.0
SWIGLU_ALPHA = 1.702

LANES = 128
SUBLANES = 8
POOL_HALO = 16
VMEM_LIMIT_BYTES = 56 * 1024 * 1024
HI_MASK = 0xFFFF0000


def _tile(n, preferred):
    if n <= preferred:
        return n
    t = preferred - preferred % LANES
    while n % t:
        t -= LANES
    assert t > 0, (n, preferred)
    return t


def _params(*semantics):
    return pltpu.CompilerParams(dimension_semantics=semantics, vmem_limit_bytes=VMEM_LIMIT_BYTES)


def _modulated_rms_norm(x, g, sc, sh):
    y = x * lax.rsqrt(jnp.mean(x * x, axis=-1, keepdims=True) + NORM_EPS)
    return (y * g) * (1.0 + sc) + sh


def _pack_halves(v):
    half = v.shape[1] // 2
    bits = lax.bitcast_convert_type(v.astype(BF16).astype(F32), U32)
    return (bits[:, :half] >> 16) | (bits[:, half:] & U32(HI_MASK))


def _unpack_halves(w):
    lo = lax.bitcast_convert_type(w << 16, F32)
    hi = lax.bitcast_convert_type(w & U32(HI_MASK), F32)
    return lo, hi


def _cond_kernel(c_ref, w_ref, b_ref, o_ref):
    z = jnp.dot(c_ref[...], w_ref[...], precision=HIGHEST, preferred_element_type=F32) + b_ref[...]
    o_ref[...] = z * jax.nn.sigmoid(z)


def _mod_kernel(cond_ref, w_ref, b_ref, o_ref):
    o_ref[0] = jnp.dot(cond_ref[...], w_ref[0], precision=HIGHEST, preferred_element_type=F32) + b_ref[0]


def _modulation(c, w_cond, b_cond, w_mod, b_mod):
    bsz, d = c.shape
    rank = w_cond.shape[1]
    depth, _, n6 = w_mod.shape
    rows = -(-bsz // SUBLANES) * SUBLANES
    c_pad = jnp.pad(c, ((0, rows - bsz), (0, 0)))
    cond = pl.pallas_call(
        _cond_kernel,
        out_shape=jax.ShapeDtypeStruct((rows, rank), F32),
        compiler_params=pltpu.CompilerParams(vmem_limit_bytes=VMEM_LIMIT_BYTES),
        name="cond",
    )(c_pad, w_cond, b_cond.reshape(1, rank))
    tn = _tile(n6, 2048)
    mod = pl.pallas_call(
        _mod_kernel,
        out_shape=jax.ShapeDtypeStruct((depth, rows, n6), F32),
        grid=(depth, n6 // tn),
        in_specs=[
            pl.BlockSpec((rows, rank), lambda l, j: (0, 0)),
            pl.BlockSpec((1, rank, tn), lambda l, j: (l, 0, j)),
            pl.BlockSpec((1, 1, tn), lambda l, j: (l, 0, j)),
        ],
        out_specs=pl.BlockSpec((1, rows, tn), lambda l, j: (l, 0, j)),
        compiler_params=_params("parallel", "parallel"),
        name="mod",
    )(cond, w_mod, b_mod.reshape(depth, 1, n6))
    return mod[:, :bsz].reshape(depth, bsz, n6 // d, 1, d)


def _norm_mm_kernel(x_ref, g_ref, sc_ref, sh_ref, w_ref, b_ref, o_ref, h_ref, *, gelu):
    @pl.when(pl.program_id(1) == 0)
    def _():
        h_ref[...] = _modulated_rms_norm(x_ref[...], g_ref[...], sc_ref[0], sh_ref[0]).astype(BF16)

    acc = jnp.dot(h_ref[...], w_ref[...], preferred_element_type=F32) + b_ref[...]
    if gelu:
        acc = 0.5 * acc * (1.0 + lax.erf(acc * F32(0.7071067811865476)))
    o_ref[...] = acc.astype(o_ref.dtype)


def _norm_matmul(x, g, sc, sh, w, b, *, seq_len, gelu):
    t, d = x.shape
    n = w.shape[1]
    tm = _tile(seq_len, 512)
    tn = _tile(n, 1024)
    per_seq = seq_len // tm
    return pl.pallas_call(
        functools.partial(_norm_mm_kernel, gelu=gelu),
        out_shape=jax.ShapeDtypeStruct((t, n), BF16),
        grid=(t // tm, n // tn),
        in_specs=[
            pl.BlockSpec((tm, d), lambda i, j: (i, 0)),
            pl.BlockSpec((1, d), lambda i, j: (0, 0)),
            pl.BlockSpec((1, 1, d), lambda i, j: (i // per_seq, 0, 0)),
            pl.BlockSpec((1, 1, d), lambda i, j: (i // per_seq, 0, 0)),
            pl.BlockSpec((d, tn), lambda i, j: (0, j)),
            pl.BlockSpec((1, tn), lambda i, j: (0, j)),
        ],
        out_specs=pl.BlockSpec((tm, tn), lambda i, j: (i, j)),
        scratch_shapes=[pltpu.VMEM((tm, d), BF16)],
        compiler_params=_params("parallel", "arbitrary"),
        name="norm_matmul_gelu" if gelu else "norm_matmul",
    )(x, g, sc, sh, w, b)


def _mm_residual_kernel(a_ref, w_ref, x_ref, gate_ref, o_ref):
    y = jnp.dot(a_ref[...], w_ref[...], preferred_element_type=F32)
    o_ref[...] = x_ref[...] + gate_ref[0] * y


def _matmul_residual(a, w, x, gate, *, seq_len):
    t, k = a.shape
    d = w.shape[1]
    tm = _tile(seq_len, 1024)
    tn = _tile(d, 512)
    per_seq = seq_len // tm
    return pl.pallas_call(
        _mm_residual_kernel,
        out_shape=jax.ShapeDtypeStruct((t, d), F32),
        grid=(t // tm, d // tn),
        in_specs=[
            pl.BlockSpec((tm, k), lambda i, j: (i, 0)),
            pl.BlockSpec((k, tn), lambda i, j: (0, j)),
            pl.BlockSpec((tm, tn), lambda i, j: (i, j)),
            pl.BlockSpec((1, 1, tn), lambda i, j: (i // per_seq, 0, j)),
        ],
        out_specs=pl.BlockSpec((tm, tn), lambda i, j: (i, j)),
        input_output_aliases={2: 0},
        compiler_params=_params("parallel", "parallel"),
        name="matmul_residual",
    )(a, w, x, gate)


def _pool_kernel(u_ref, halo_ref, wg_ref, scale_ref, o_ref, ext_ref, *, seq_len):
    tm = u_ref.shape[0]
    gw = wg_ref.shape[1]
    start = lax.rem(pl.program_id(0) * tm, seq_len)
    ext_ref[0:POOL_HALO] = jnp.where(start == 0, 0.0, halo_ref[...].astype(F32))
    ext_ref[POOL_HALO:] = u_ref[...].astype(F32)
    pos = start + lax.broadcasted_iota(I32, (tm, 1), 0)
    for g, window in enumerate(POOL_WINDOWS):
        cols = slice(g * gw, (g + 1) * gw)
        cur = ext_ref[POOL_HALO:POOL_HALO + tm, cols]
        total = cur
        for lag in range(1, window):
            total = total + ext_ref[POOL_HALO - lag:POOL_HALO - lag + tm, cols]
        inv_cnt = 1.0 / jnp.minimum(pos + 1, window).astype(F32)
        pooled = total * inv_cnt - cur
        mixed = jnp.dot(pooled.astype(BF16), wg_ref[g], preferred_element_type=F32)
        o_ref[:, cols] = (mixed * scale_ref[:, cols]).astype(o_ref.dtype)


def _pool_middle(u, w_group, scale, *, seq_len):
    t, d = u.shape
    assert w_group.shape[0] == len(POOL_WINDOWS) and max(POOL_WINDOWS) <= POOL_HALO
    tm = _tile(seq_len, 512)
    halo_blocks = tm // POOL_HALO
    return pl.pallas_call(
        functools.partial(_pool_kernel, seq_len=seq_len),
        out_shape=jax.ShapeDtypeStruct((t, d), BF16),
        grid=(t // tm,),
        in_specs=[
            pl.BlockSpec((tm, d), lambda i: (i, 0)),
            pl.BlockSpec((POOL_HALO, d), lambda i: (jnp.maximum(i * halo_blocks - 1, 0), 0)),
            pl.BlockSpec(w_group.shape, lambda i: (0, 0, 0)),
            pl.BlockSpec((1, d), lambda i: (0, 0)),
        ],
        out_specs=pl.BlockSpec((tm, d), lambda i: (i, 0)),
        scratch_shapes=[pltpu.VMEM((tm + POOL_HALO, d), F32)],
        compiler_params=_params("parallel"),
        name="pool_middle",
    )(u, u, w_group, scale)


def _sgu_kernel(u_ref, v_ref, lng_ref, lnb_ref, ws_ref, bst_ref, o_ref, vn_ref):
    tm, width = u_ref.shape
    n_heads, chunk, _ = ws_ref.shape
    hw = width // n_heads
    v = v_ref[...].astype(F32)
    vc = v - jnp.mean(v, axis=-1, keepdims=True)
    vn = vc * lax.rsqrt(jnp.mean(vc * vc, axis=-1, keepdims=True) + NORM_EPS)
    vn_ref[...] = (vn * lng_ref[...] + lnb_ref[...]).astype(BF16)
    causal = (lax.broadcasted_iota(I32, (chunk, chunk), 0) >= lax.broadcasted_iota(I32, (chunk, chunk), 1))
    for h in range(n_heads):
        w_causal = jnp.where(causal, ws_ref[h], 0.0).astype(BF16)
        bias = bst_ref[:, h:h + 1]
        cols = slice(h * hw, (h + 1) * hw)
        for c in range(tm // chunk):
            rows = slice(c * chunk, (c + 1) * chunk)
            mixed = jnp.dot(w_causal, vn_ref[rows, cols], preferred_element_type=F32) + bias
            o_ref[rows, cols] = (u_ref[rows, cols].astype(F32) * mixed).astype(o_ref.dtype)


def _sgu_middle(uv, ln_g, ln_b, w_s, b_s_t, *, seq_len):
    t, two_w = uv.shape
    width = two_w // 2
    chunk = w_s.shape[1]
    tm = _tile(seq_len, 4 * chunk)
    return pl.pallas_call(
        _sgu_kernel,
        out_shape=jax.ShapeDtypeStruct((t, width), BF16),
        grid=(t // tm,),
        in_specs=[
            pl.BlockSpec((tm, width), lambda i: (i, 0)),
            pl.BlockSpec((tm, width), lambda i: (i, 1)),
            pl.BlockSpec((1, width), lambda i: (0, 0)),
            pl.BlockSpec((1, width), lambda i: (0, 0)),
            pl.BlockSpec(w_s.shape, lambda i: (0, 0, 0)),
            pl.BlockSpec(b_s_t.shape, lambda i: (0, 0)),
        ],
        out_specs=pl.BlockSpec((tm, width), lambda i: (i, 0)),
        scratch_shapes=[pltpu.VMEM((tm, width), BF16)],
        compiler_params=_params("parallel"),
        name="sgu_middle",
    )(uv, uv, ln_g, ln_b, w_s, b_s_t)


def _router_kernel(x_ref, g_ref, sc_ref, sh_ref, wr_ref, br_ref,
                   hp_ref, gates_ref, eid_ref, rank_ref, counts_ref, carry_ref, *, n_experts):
    tm = x_ref.shape[0]

    @pl.when(pl.program_id(0) == 0)
    def _():
        carry_ref[...] = jnp.zeros_like(carry_ref)

    h = _modulated_rms_norm(x_ref[...], g_ref[...], sc_ref[0], sh_ref[0])
    hp_ref[...] = _pack_halves(h)

    logits = jnp.dot(h, wr_ref[...], precision=HIGHEST, preferred_element_type=F32) + br_ref[...]
    lane = lax.broadcasted_iota(I32, (tm, LANES), 1)
    remaining = jnp.where(lane < n_experts, logits, -jnp.inf)
    selected = jnp.zeros((tm, LANES), F32)
    top_vals, top_ids = [], []
    for _ in range(TOP_K):
        best = jnp.max(remaining, axis=-1, keepdims=True)
        idx = jnp.min(jnp.where(remaining == best, lane, LANES), axis=-1, keepdims=True)
        hit = lane == idx
        selected = jnp.where(hit, 1.0, selected)
        remaining = jnp.where(hit, -jnp.inf, remaining)
        top_vals.append(best)
        top_ids.append(idx)
    exps = [jnp.exp(v - top_vals[0]) for v in top_vals]
    denom = exps[0]
    for e in exps[1:]:
        denom = denom + e

    lower = (lax.broadcasted_iota(I32, (tm, tm), 0) >= lax.broadcasted_iota(I32, (tm, tm), 1))
    incl = jnp.dot(lower.astype(BF16), selected.astype(BF16), preferred_element_type=F32)
    before = incl - selected + carry_ref[...]
    gates = jnp.zeros((tm, LANES), F32)
    eids = jnp.zeros((tm, LANES), I32)
    ranks = jnp.zeros((tm, LANES), I32)
    for k in range(TOP_K):
        rank_k = jnp.sum(jnp.where(lane == top_ids[k], before, 0.0), axis=-1, keepdims=True)
        gates = jnp.where(lane == k, exps[k] / denom, gates)
        eids = jnp.where(lane == k, top_ids[k], eids)
        ranks = jnp.where(lane == k, rank_k.astype(I32), ranks)
    gates_ref[...] = gates
    eid_ref[...] = eids
    rank_ref[...] = ranks
    carry_ref[...] = carry_ref[...] + incl[tm - 1:tm, :]
    counts_ref[...] = carry_ref[...].astype(I32)


def _route(x, g, sc, sh, w_router_pad, b_router_pad, *, seq_len, n_experts):
    t, d = x.shape
    tm = _tile(seq_len, 512)
    per_seq = seq_len // tm
    tok_spec = pl.BlockSpec((tm, LANES), lambda i: (i, 0))
    return pl.pallas_call(
        functools.partial(_router_kernel, n_experts=n_experts),
        out_shape=[
            jax.ShapeDtypeStruct((t, d // 2), U32),
            jax.ShapeDtypeStruct((t, LANES), F32),
            jax.ShapeDtypeStruct((t, LANES), I32),
            jax.ShapeDtypeStruct((t, LANES), I32),
            jax.ShapeDtypeStruct((1, LANES), I32),
        ],
        grid=(t // tm,),
        in_specs=[
            pl.BlockSpec((tm, d), lambda i: (i, 0)),
            pl.BlockSpec((1, d), lambda i: (0, 0)),
            pl.BlockSpec((1, 1, d), lambda i: (i // per_seq, 0, 0)),
            pl.BlockSpec((1, 1, d), lambda i: (i // per_seq, 0, 0)),
            pl.BlockSpec((d, LANES), lambda i: (0, 0)),
            pl.BlockSpec((1, LANES), lambda i: (0, 0)),
        ],
        out_specs=[
            pl.BlockSpec((tm, d // 2), lambda i: (i, 0)),
            tok_spec, tok_spec, tok_spec,
            pl.BlockSpec((1, LANES), lambda i: (0, 0)),
        ],
        scratch_shapes=[pltpu.VMEM((1, LANES), F32)],
        compiler_params=_params("arbitrary"),
        name="moe_router",
    )(x, g, sc, sh, w_router_pad, b_router_pad)


def _dispatch_kernel(dest_ref, hp_ref, xs_in_ref, xs_ref, sem):
    del xs_in_ref
    tm = hp_ref.shape[0]
    base = pl.program_id(0) * tm

    def issue(t, carry):
        for k in range(TOP_K):
            d = dest_ref[(base + t) * TOP_K + k]
            pltpu.make_async_copy(hp_ref.at[pl.ds(t, 1)], xs_ref.at[pl.ds(d, 1)], sem).start()
        return carry

    lax.fori_loop(0, tm, issue, 0)
    for _ in range(TOP_K):
        pltpu.make_async_copy(hp_ref, xs_ref.at[pl.ds(0, tm)], sem).wait()


def _dispatch(dest, hp, n_rows):
    t, half = hp.shape
    tm = _tile(t, 256)
    xs0 = jnp.zeros((n_rows, half), U32)
    return pl.pallas_call(
        _dispatch_kernel,
        out_shape=jax.ShapeDtypeStruct((n_rows, half), U32),
        grid_spec=pltpu.PrefetchScalarGridSpec(
            num_scalar_prefetch=1,
            grid=(t // tm,),
            in_specs=[
                pl.BlockSpec((tm, half), lambda i, dest: (i, 0)),
                pl.BlockSpec(memory_space=pl.ANY),
            ],
            out_specs=pl.BlockSpec(memory_space=pl.ANY),
            scratch_shapes=[pltpu.SemaphoreType.DMA(())],
        ),
        input_output_aliases={2: 0},
        compiler_params=_params("arbitrary"),
        name="moe_dispatch",
    )(dest, hp, xs0)


def _expert_kernel(blk_expert_ref, n_used_ref, xs_ref, wu_ref, bu_ref, wd_ref, bd_ref, y_ref):
    del blk_expert_ref
    half = xs_ref.shape[1]
    f = wd_ref.shape[1]
    live = pl.program_id(0) < n_used_ref[0]

    @pl.when(live)
    def _():
        lo, hi = _unpack_halves(xs_ref[...])
        hu = (jnp.dot(lo.astype(BF16), wu_ref[0, :half], preferred_element_type=F32)
              + jnp.dot(hi.astype(BF16), wu_ref[0, half:], preferred_element_type=F32)
              + bu_ref[0])
        glu = jnp.minimum(hu[:, :f], SWIGLU_LIMIT)
        lin = jnp.clip(hu[:, f:], -SWIGLU_LIMIT, SWIGLU_LIMIT)
        act = glu * jax.nn.sigmoid(SWIGLU_ALPHA * glu) * (lin + 1.0)
        y = jnp.dot(act.astype(BF16), wd_ref[0], preferred_element_type=F32) + bd_ref[0]
        y_ref[...] = _pack_halves(y)

    @pl.when(jnp.logical_not(live))
    def _():
        y_ref[...] = jnp.zeros_like(y_ref)


def _experts(blk_expert, n_used, xs, w_up, b_up, w_down, b_down, *, block_rows):
    n_rows, half = xs.shape
    n_experts, d, f2 = w_up.shape
    f = f2 // 2
    n_blocks = n_rows // block_rows

    def row_block(b, blk_expert, n_used):
        return (jnp.minimum(b, n_used[0] - 1), 0)

    def expert_block(b, blk_expert, n_used):
        return (blk_expert[b], 0, 0)

    return pl.pallas_call(
        _expert_kernel,
        out_shape=jax.ShapeDtypeStruct((n_rows, half), U32),
        grid_spec=pltpu.PrefetchScalarGridSpec(
            num_scalar_prefetch=2,
            grid=(n_blocks,),
            in_specs=[
                pl.BlockSpec((block_rows, half), row_block),
                pl.BlockSpec((1, d, f2), expert_block),
                pl.BlockSpec((1, 1, f2), expert_block),
                pl.BlockSpec((1, f, d), expert_block),
                pl.BlockSpec((1, 1, d), expert_block),
            ],
            out_specs=pl.BlockSpec((block_rows, half), lambda b, blk_expert, n_used: (b, 0)),
        ),
        compiler_params=_params("arbitrary"),
        name="moe_experts",
    )(blk_expert, n_used, xs, w_up, b_up.reshape(n_experts, 1, f2), w_down, b_down.reshape(n_experts, 1, d))


def _combine_kernel(dest_ref, yb_ref, x_ref, gates_ref, gt_ref, o_ref, buf_ref, sem):
    tm = x_ref.shape[0]
    half = buf_ref.shape[2]
    base = pl.program_id(0) * tm

    def issue(t, carry):
        for k in range(TOP_K):
            d = dest_ref[(base + t) * TOP_K + k]
            pltpu.make_async_copy(yb_ref.at[pl.ds(d, 1)], buf_ref.at[k, pl.ds(t, 1)], sem).start()
        return carry

    lax.fori_loop(0, tm, issue, 0)
    for k in range(TOP_K):
        pltpu.make_async_copy(yb_ref.at[pl.ds(0, tm)], buf_ref.at[k], sem).wait()

    acc_lo = jnp.zeros((tm, half), F32)
    acc_hi = jnp.zeros((tm, half), F32)
    for k in range(TOP_K):
        lo, hi = _unpack_halves(buf_ref[k])
        gate = gates_ref[:, k:k + 1]
        acc_lo = acc_lo + gate * lo
        acc_hi = acc_hi + gate * hi
    o_ref[:, :half] = x_ref[:, :half] + gt_ref[0, :, :half] * acc_lo
    o_ref[:, half:] = x_ref[:, half:] + gt_ref[0, :, half:] * acc_hi


def _combine(dest, yb, x, gates, gt, *, seq_len):
    t, d = x.shape
    half = d // 2
    tm = _tile(seq_len, 256)
    per_seq = seq_len // tm
    return pl.pallas_call(
        _combine_kernel,
        out_shape=jax.ShapeDtypeStruct((t, d), F32),
        grid_spec=pltpu.PrefetchScalarGridSpec(
            num_scalar_prefetch=1,
            grid=(t // tm,),
            in_specs=[
                pl.BlockSpec(memory_space=pl.ANY),
                pl.BlockSpec((tm, d), lambda i, dest: (i, 0)),
                pl.BlockSpec((tm, LANES), lambda i, dest: (i, 0)),
                pl.BlockSpec((1, 1, d), lambda i, dest: (i // per_seq, 0, 0)),
            ],
            out_specs=pl.BlockSpec((tm, d), lambda i, dest: (i, 0)),
            scratch_shapes=[pltpu.VMEM((TOP_K, tm, half), U32), pltpu.SemaphoreType.DMA(())],
        ),
        input_output_aliases={2: 0},
        compiler_params=_params("arbitrary"),
        name="moe_combine",
    )(dest, yb, x, gates, gt)


def _moe_layer(x, g, sc, sh, gt, w_router, b_router, w_up, b_up, w_down, b_down, *, seq_len, block_rows):
    t, d = x.shape
    n_experts = w_router.shape[1]
    assert n_experts <= LANES
    w_router_pad = jnp.pad(w_router, ((0, 0), (0, LANES - n_experts)))
    b_router_pad = jnp.pad(b_router, (0, LANES - n_experts)).reshape(1, LANES)
    hp, gates, eids, ranks, counts = _route(x, g, sc, sh, w_router_pad, b_router_pad,
                                            seq_len=seq_len, n_experts=n_experts)
    counts = counts[0, :n_experts]
    padded = (counts + block_rows - 1) // block_rows * block_rows
    seg_end = jnp.cumsum(padded)
    seg_start = seg_end - padded
    n_rows = t * TOP_K + n_experts * block_rows
    n_blocks = n_rows // block_rows
    dest = (seg_start[eids[:, :TOP_K]] + ranks[:, :TOP_K]).reshape(-1).astype(I32)
    blk_start = jnp.arange(n_blocks, dtype=I32) * block_rows
    blk_expert = jnp.minimum(jnp.searchsorted(seg_end, blk_start, side="right"), n_experts - 1).astype(I32)
    n_used = (seg_end[-1:] // block_rows).astype(I32)

    xs = _dispatch(dest, hp, n_rows)
    yb = _experts(blk_expert, n_used, xs, w_up, b_up, w_down, b_down, block_rows=block_rows)
    return _combine(dest, yb, x, gates, gt, seq_len=seq_len)


def _final_norm_kernel(x_ref, g_ref, o_ref):
    x = x_ref[...]
    o_ref[...] = x * lax.rsqrt(jnp.mean(x * x, axis=-1, keepdims=True) + NORM_EPS) * g_ref[...]


def _final_norm(x, g):
    t, d = x.shape
    tm = _tile(t, 512)
    return pl.pallas_call(
        _final_norm_kernel,
        out_shape=jax.ShapeDtypeStruct((t, d), F32),
        grid=(t // tm,),
        in_specs=[pl.BlockSpec((tm, d), lambda i: (i, 0)), pl.BlockSpec((1, d), lambda i: (0, 0))],
        out_specs=pl.BlockSpec((tm, d), lambda i: (i, 0)),
        compiler_params=_params("parallel"),
        name="final_norm",
    )(x, g)


def kernel(x, c, w_cond, b_cond, w_mod, b_mod, g_norm_mix, g_norm_ffn, pool_w_in, pool_w_group, pool_scale, pool_w_out, gmlp_w_in, gmlp_b_in, gmlp_ln_g, gmlp_ln_b, gmlp_w_s, gmlp_b_s, gmlp_w_out, moe_w_router, moe_b_router, moe_w_up, moe_b_up, moe_w_down, moe_b_down, g_final):
    bsz, seq_len, d = x.shape
    depth = w_mod.shape[0]
    mod = _modulation(c, w_cond, b_cond, w_mod, b_mod)
    xt = x.reshape(bsz * seq_len, d)
    zero_bias = jnp.zeros((1, d), F32)
    for layer in range(depth):
        sh1, sc1, gt1, sh2, sc2, gt2 = (mod[layer, :, m] for m in range(6))
        j = layer // 2
        g_mix = g_norm_mix[layer].reshape(1, d)
        if layer % 2 == 0:
            u = _norm_matmul(xt, g_mix, sc1, sh1, pool_w_in[j].astype(BF16), zero_bias,
                             seq_len=seq_len, gelu=False)
            mid = _pool_middle(u, pool_w_group[j].astype(BF16), pool_scale[j].reshape(1, d), seq_len=seq_len)
            w_out = pool_w_out[j]
        else:
            uv = _norm_matmul(xt, g_mix, sc1, sh1, gmlp_w_in[j].astype(BF16), gmlp_b_in[j].reshape(1, -1),
                              seq_len=seq_len, gelu=True)
            mid = _sgu_middle(uv, gmlp_ln_g[j].reshape(1, -1), gmlp_ln_b[j].reshape(1, -1),
                              gmlp_w_s[j], gmlp_b_s[j].T, seq_len=seq_len)
            w_out = gmlp_w_out[j]
        xt = _matmul_residual(mid, w_out.astype(BF16), xt, gt1, seq_len=seq_len)
        xt = _moe_layer(xt, g_norm_ffn[layer].reshape(1, d), sc2, sh2, gt2,
                        moe_w_router[layer], moe_b_router[layer],
                        moe_w_up[layer].astype(BF16), moe_b_up[layer],
                        moe_w_down[layer].astype(BF16), moe_b_down[layer],
                        seq_len=seq_len, block_rows=256)
    return _final_norm(xt, g_final.reshape(1, d)).reshape(bsz, seq_len, d)
```

```python
import functools

import jax
import jax.numpy as jnp
from jax import lax
from jax.experimental import pallas as pl
from jax.experimental.pallas import tpu as pltpu

F32 = jnp.float32
BF16 = jnp.bfloat16
U32 = jnp.uint32
I32 = jnp.int32
HIGHEST = lax.Precision.HIGHEST

NORM_EPS = 1e-5
POOL_WINDOWS = (2, 4, 8, 16)
TOP_K = 4
SWIGLU_LIMIT = 7.0
SWIGLU_ALPHA = 1.702

LANES = 128
SUBLANES = 8
POOL_HALO = 16
ZERO_CHUNK_ROWS = 64
DMA_ISSUE_UNROLL = 8
VMEM_LIMIT_BYTES = 56 * 1024 * 1024
HI_MASK = 0xFFFF0000


def _tile(n, preferred):
    if n <= preferred:
        return n
    t = preferred - preferred % LANES
    while n % t:
        t -= LANES
    assert t > 0, (n, preferred)
    return t


def _params(*semantics):
    return pltpu.CompilerParams(dimension_semantics=semantics, vmem_limit_bytes=VMEM_LIMIT_BYTES)


def _modulated_rms_norm(x, g, sc, sh):
    y = x * lax.rsqrt(jnp.mean(x * x, axis=-1, keepdims=True) + NORM_EPS)
    return (y * g) * (1.0 + sc) + sh


def _pack_halves(v):
    half = v.shape[1] // 2
    bits = lax.bitcast_convert_type(v.astype(BF16).astype(F32), U32)
    return (bits[:, :half] >> 16) | (bits[:, half:] & U32(HI_MASK))


def _unpack_halves(w):
    lo = lax.bitcast_convert_type(w << 16, F32)
    hi = lax.bitcast_convert_type(w & U32(HI_MASK), F32)
    return lo, hi


def _cond_kernel(c_ref, w_ref, b_ref, o_ref):
    z = jnp.dot(c_ref[...], w_ref[...], precision=HIGHEST, preferred_element_type=F32) + b_ref[...]
    o_ref[...] = z * jax.nn.sigmoid(z)


def _mod_kernel(cond_ref, w_ref, b_ref, o_ref):
    o_ref[0] = jnp.dot(cond_ref[...], w_ref[0], precision=HIGHEST, preferred_element_type=F32) + b_ref[0]


def _modulation(c, w_cond, b_cond, w_mod, b_mod):
    bsz, d = c.shape
    rank = w_cond.shape[1]
    depth, _, n6 = w_mod.shape
    rows = -(-bsz // SUBLANES) * SUBLANES
    c_pad = jnp.pad(c, ((0, rows - bsz), (0, 0)))
    cond = pl.pallas_call(
        _cond_kernel,
        out_shape=jax.ShapeDtypeStruct((rows, rank), F32),
        compiler_params=pltpu.CompilerParams(vmem_limit_bytes=VMEM_LIMIT_BYTES),
        name="cond",
    )(c_pad, w_cond, b_cond.reshape(1, rank))
    tn = _tile(n6, 2048)
    mod = pl.pallas_call(
        _mod_kernel,
        out_shape=jax.ShapeDtypeStruct((depth, rows, n6), F32),
        grid=(depth, n6 // tn),
        in_specs=[
            pl.BlockSpec((rows, rank), lambda l, j: (0, 0)),
            pl.BlockSpec((1, rank, tn), lambda l, j: (l, 0, j)),
            pl.BlockSpec((1, 1, tn), lambda l, j: (l, 0, j)),
        ],
        out_specs=pl.BlockSpec((1, rows, tn), lambda l, j: (l, 0, j)),
        compiler_params=_params("parallel", "parallel"),
        name="mod",
    )(cond, w_mod, b_mod.reshape(depth, 1, n6))
    return mod[:, :bsz].reshape(depth, bsz, n6 // d, 1, d)


def _cast_kernel(w_ref, o_ref):
    o_ref[...] = w_ref[0].astype(o_ref.dtype)


def _bf16_weight(w, index):
    _, rows, cols = w.shape
    tr = _tile(rows, 512)
    return pl.pallas_call(
        _cast_kernel,
        out_shape=jax.ShapeDtypeStruct((rows, cols), BF16),
        grid=(rows // tr,),
        in_specs=[pl.BlockSpec((1, tr, cols), lambda i: (index, i, 0))],
        out_specs=pl.BlockSpec((tr, cols), lambda i: (i, 0)),
        compiler_params=_params("parallel"),
        name="weight_to_bf16",
    )(w)


def _norm_mm_kernel(x_ref, g_ref, sc_ref, sh_ref, w_ref, b_ref, o_ref, h_ref, *, gelu):
    @pl.when(pl.program_id(1) == 0)
    def _():
        h_ref[...] = _modulated_rms_norm(x_ref[...], g_ref[...], sc_ref[0], sh_ref[0]).astype(BF16)

    acc = jnp.dot(h_ref[...], w_ref[...], preferred_element_type=F32) + b_ref[...]
    if gelu:
        acc = 0.5 * acc * (1.0 + lax.erf(acc * F32(0.7071067811865476)))
    o_ref[...] = acc.astype(o_ref.dtype)


def _norm_matmul(x, g, sc, sh, w, b, *, seq_len, gelu):
    t, d = x.shape
    n = w.shape[1]
    tm = _tile(seq_len, 512)
    tn = _tile(n, 1024)
    per_seq = seq_len // tm
    return pl.pallas_call(
        functools.partial(_norm_mm_kernel, gelu=gelu),
        out_shape=jax.ShapeDtypeStruct((t, n), BF16),
        grid=(t // tm, n // tn),
        in_specs=[
            pl.BlockSpec((tm, d), lambda i, j: (i, 0)),
            pl.BlockSpec((1, d), lambda i, j: (0, 0)),
            pl.BlockSpec((1, 1, d), lambda i, j: (i // per_seq, 0, 0)),
            pl.BlockSpec((1, 1, d), lambda i, j: (i // per_seq, 0, 0)),
            pl.BlockSpec((d, tn), lambda i, j: (0, j)),
            pl.BlockSpec((1, tn), lambda i, j: (0, j)),
        ],
        out_specs=pl.BlockSpec((tm, tn), lambda i, j: (i, j)),
        scratch_shapes=[pltpu.VMEM((tm, d), BF16)],
        compiler_params=_params("parallel", "arbitrary"),
        name="norm_matmul_gelu" if gelu else "norm_matmul",
    )(x, g, sc, sh, w, b)


def _mm_residual_kernel(a_ref, w_ref, x_ref, gate_ref, o_ref):
    y = jnp.dot(a_ref[...], w_ref[...], preferred_element_type=F32)
    o_ref[...] = x_ref[...] + gate_ref[0] * y


def _matmul_residual(a, w, x, gate, *, seq_len):
    t, k = a.shape
    d = w.shape[1]
    tm = _tile(seq_len, 1024)
    tn = _tile(d, 512)
    per_seq = seq_len // tm
    return pl.pallas_call(
        _mm_residual_kernel,
        out_shape=jax.ShapeDtypeStruct((t, d), F32),
        grid=(t // tm, d // tn),
        in_specs=[
            pl.BlockSpec((tm, k), lambda i, j: (i, 0)),
            pl.BlockSpec((k, tn), lambda i, j: (0, j)),
            pl.BlockSpec((tm, tn), lambda i, j: (i, j)),
            pl.BlockSpec((1, 1, tn), lambda i, j: (i // per_seq, 0, j)),
        ],
        out_specs=pl.BlockSpec((tm, tn), lambda i, j: (i, j)),
        compiler_params=_params("parallel", "parallel"),
        name="matmul_residual",
    )(a, w, x, gate)


def _pool_kernel(u_ref, halo_ref, wg_ref, scale_ref, o_ref, ext_ref, *, seq_len):
    tm = u_ref.shape[0]
    gw = wg_ref.shape[1]
    start = lax.rem(pl.program_id(0) * tm, seq_len)
    ext_ref[0:POOL_HALO] = jnp.where(start == 0, 0.0, halo_ref[...].astype(F32))
    ext_ref[POOL_HALO:] = u_ref[...].astype(F32)
    pos = start + lax.broadcasted_iota(I32, (tm, 1), 0)
    for g, window in enumerate(POOL_WINDOWS):
        cols = slice(g * gw, (g + 1) * gw)
        cur = ext_ref[POOL_HALO:POOL_HALO + tm, cols]
        total = cur
        for lag in range(1, window):
            total = total + ext_ref[POOL_HALO - lag:POOL_HALO - lag + tm, cols]
        inv_cnt = 1.0 / jnp.minimum(pos + 1, window).astype(F32)
        pooled = total * inv_cnt - cur
        mixed = jnp.dot(pooled.astype(BF16), wg_ref[g], preferred_element_type=F32)
        o_ref[:, cols] = (mixed * scale_ref[:, cols]).astype(o_ref.dtype)


def _pool_middle(u, w_group, scale, *, seq_len):
    t, d = u.shape
    assert w_group.shape[0] == len(POOL_WINDOWS) and max(POOL_WINDOWS) <= POOL_HALO
    tm = _tile(seq_len, 512)
    halo_blocks = tm // POOL_HALO
    return pl.pallas_call(
        functools.partial(_pool_kernel, seq_len=seq_len),
        out_shape=jax.ShapeDtypeStruct((t, d), BF16),
        grid=(t // tm,),
        in_specs=[
            pl.BlockSpec((tm, d), lambda i: (i, 0)),
            pl.BlockSpec((POOL_HALO, d), lambda i: (jnp.maximum(i * halo_blocks - 1, 0), 0)),
            pl.BlockSpec(w_group.shape, lambda i: (0, 0, 0)),
            pl.BlockSpec((1, d), lambda i: (0, 0)),
        ],
        out_specs=pl.BlockSpec((tm, d), lambda i: (i, 0)),
        scratch_shapes=[pltpu.VMEM((tm + POOL_HALO, d), F32)],
        compiler_params=_params("parallel"),
        name="pool_middle",
    )(u, u, w_group, scale)


def _sgu_kernel(u_ref, v_ref, lng_ref, lnb_ref, ws_ref, bst_ref, o_ref, vn_ref):
    tm, width = u_ref.shape
    n_heads, chunk, _ = ws_ref.shape
    hw = width // n_heads
    v = v_ref[...].astype(F32)
    vc = v - jnp.mean(v, axis=-1, keepdims=True)
    vn = vc * lax.rsqrt(jnp.mean(vc * vc, axis=-1, keepdims=True) + NORM_EPS)
    vn_ref[...] = (vn * lng_ref[...] + lnb_ref[...]).astype(BF16)
    causal = (lax.broadcasted_iota(I32, (chunk, chunk), 0) >= lax.broadcasted_iota(I32, (chunk, chunk), 1))
    for h in range(n_heads):
        w_causal = jnp.where(causal, ws_ref[h], 0.0).astype(BF16)
        bias = bst_ref[:, h:h + 1]
        cols = slice(h * hw, (h + 1) * hw)
        for c in range(tm // chunk):
            rows = slice(c * chunk, (c + 1) * chunk)
            mixed = jnp.dot(w_causal, vn_ref[rows, cols], preferred_element_type=F32) + bias
            o_ref[rows, cols] = (u_ref[rows, cols].astype(F32) * mixed).astype(o_ref.dtype)


def _sgu_middle(uv, ln_g, ln_b, w_s, b_s_t, *, seq_len):
    t, two_w = uv.shape
    width = two_w // 2
    chunk = w_s.shape[1]
    tm = _tile(seq_len, 4 * chunk)
    return pl.pallas_call(
        _sgu_kernel,
        out_shape=jax.ShapeDtypeStruct((t, width), BF16),
        grid=(t // tm,),
        in_specs=[
            pl.BlockSpec((tm, width), lambda i: (i, 0)),
            pl.BlockSpec((tm, width), lambda i: (i, 1)),
            pl.BlockSpec((1, width), lambda i: (0, 0)),
            pl.BlockSpec((1, width), lambda i: (0, 0)),
            pl.BlockSpec(w_s.shape, lambda i: (0, 0, 0)),
            pl.BlockSpec(b_s_t.shape, lambda i: (0, 0)),
        ],
        out_specs=pl.BlockSpec((tm, width), lambda i: (i, 0)),
        scratch_shapes=[pltpu.VMEM((tm, width), BF16)],
        compiler_params=_params("parallel"),
        name="sgu_middle",
    )(uv, uv, ln_g, ln_b, w_s, b_s_t)


def _router_kernel(x_ref, g_ref, sc_ref, sh_ref, wr_ref, br_ref,
                   hp_ref, gates_ref, eid_ref, rank_ref, counts_ref, carry_ref, *, n_experts):
    tm = x_ref.shape[0]

    @pl.when(pl.program_id(0) == 0)
    def _():
        carry_ref[...] = jnp.zeros_like(carry_ref)

    h = _modulated_rms_norm(x_ref[...], g_ref[...], sc_ref[0], sh_ref[0])
    hp_ref[...] = _pack_halves(h)

    lane = lax.broadcasted_iota(I32, (tm, LANES), 1)
    w_lane = lax.broadcasted_iota(I32, wr_ref.shape, 1)
    w = wr_ref[...]
    w_hi = w.astype(BF16)
    w_lo = (w - w_hi.astype(F32)).astype(BF16)
    h_hi = h.astype(BF16)
    h_lo = (h - h_hi.astype(F32)).astype(BF16)
    first = jnp.dot(h_hi, jnp.where(w_lane < n_experts, w_hi, w_lo), preferred_element_type=F32)
    second = jnp.dot(h_lo, w_hi, preferred_element_type=F32)
    logits = first + second + pltpu.roll(first, LANES - n_experts, 1) + br_ref[...]
    remaining = jnp.where(lane < n_experts, logits, -jnp.inf)
    selected = jnp.zeros((tm, LANES), F32)
    top_vals, top_ids = [], []
    for _ in range(TOP_K):
        best = jnp.max(remaining, axis=-1, keepdims=True)
        idx = jnp.min(jnp.where(remaining == best, lane, LANES), axis=-1, keepdims=True)
        hit = lane == idx
        selected = jnp.where(hit, 1.0, selected)
        remaining = jnp.where(hit, -jnp.inf, remaining)
        top_vals.append(best)
        top_ids.append(idx)
    exps = [jnp.exp(v - top_vals[0]) for v in top_vals]
    denom = exps[0]
    for e in exps[1:]:
        denom = denom + e

    lower = (lax.broadcasted_iota(I32, (tm, tm), 0) >= lax.broadcasted_iota(I32, (tm, tm), 1))
    incl = jnp.dot(lower.astype(BF16), selected.astype(BF16), preferred_element_type=F32)
    before = incl - selected + carry_ref[...]
    gates = jnp.zeros((tm, LANES), F32)
    eids = jnp.zeros((tm, LANES), I32)
    ranks = jnp.zeros((tm, LANES), I32)
    for k in range(TOP_K):
        rank_k = jnp.sum(jnp.where(lane == top_ids[k], before, 0.0), axis=-1, keepdims=True)
        gates = jnp.where(lane == k, exps[k] / denom, gates)
        eids = jnp.where(lane == k, top_ids[k], eids)
        ranks = jnp.where(lane == k, rank_k.astype(I32), ranks)
    gates_ref[...] = gates
    eid_ref[...] = eids
    rank_ref[...] = ranks
    carry_ref[...] = carry_ref[...] + incl[tm - 1:tm, :]
    counts_ref[...] = carry_ref[...].astype(I32)


def _route(x, g, sc, sh, w_router_pad, b_router_pad, *, seq_len, n_experts):
    t, d = x.shape
    tm = _tile(seq_len, 512)
    per_seq = seq_len // tm
    tok_spec = pl.BlockSpec((tm, LANES), lambda i: (i, 0))
    return pl.pallas_call(
        functools.partial(_router_kernel, n_experts=n_experts),
        out_shape=[
            jax.ShapeDtypeStruct((t, d // 2), U32),
            jax.ShapeDtypeStruct((t, LANES), F32),
            jax.ShapeDtypeStruct((t, LANES), I32),
            jax.ShapeDtypeStruct((t, LANES), I32),
            jax.ShapeDtypeStruct((1, LANES), I32),
        ],
        grid=(t // tm,),
        in_specs=[
            pl.BlockSpec((tm, d), lambda i: (i, 0)),
            pl.BlockSpec((1, d), lambda i: (0, 0)),
            pl.BlockSpec((1, 1, d), lambda i: (i // per_seq, 0, 0)),
            pl.BlockSpec((1, 1, d), lambda i: (i // per_seq, 0, 0)),
            pl.BlockSpec((d, LANES), lambda i: (0, 0)),
            pl.BlockSpec((1, LANES), lambda i: (0, 0)),
        ],
        out_specs=[
            pl.BlockSpec((tm, d // 2), lambda i: (i, 0)),
            tok_spec, tok_spec, tok_spec,
            pl.BlockSpec((1, LANES), lambda i: (0, 0)),
        ],
        scratch_shapes=[pltpu.VMEM((1, LANES), F32)],
        compiler_params=_params("arbitrary"),
        name="moe_router",
    )(x, g, sc, sh, w_router_pad, b_router_pad)


def _dispatch_kernel(dest_ref, pad_lo_ref, pad_hi_ref, hp_ref, xs_ref, zero_ref, sem, pad_sem):
    tm = hp_ref.shape[0]
    base = pl.program_id(0) * tm

    @pl.when(pl.program_id(0) == 0)
    def _():
        zero_ref[...] = jnp.zeros_like(zero_ref)
        chunk = zero_ref.shape[0]

        def chunk_copy(hi, i):
            start = pl.multiple_of(hi - (i + 1) * chunk, chunk)
            return pltpu.make_async_copy(zero_ref, xs_ref.at[pl.ds(start, chunk)], pad_sem)

        def row_copy(r):
            return pltpu.make_async_copy(zero_ref.at[pl.ds(0, 1)], xs_ref.at[pl.ds(r, 1)], pad_sem)

        def for_each_copy(act):
            for e in range(pad_lo_ref.shape[0]):
                lo, hi = pad_lo_ref[e], pad_hi_ref[e]
                n_chunks = (hi - lo) // chunk
                lax.fori_loop(0, n_chunks, lambda i, c: (act(chunk_copy(hi, i)), c)[1], 0)
                lax.fori_loop(lo, hi - n_chunks * chunk, lambda r, c: (act(row_copy(r)), c)[1], 0)

        for_each_copy(lambda cp: cp.start())
        for_each_copy(lambda cp: cp.wait())

    def issue(t, carry):
        for k in range(TOP_K):
            d = dest_ref[(base + t) * TOP_K + k]
            pltpu.make_async_copy(hp_ref.at[pl.ds(t, 1)], xs_ref.at[pl.ds(d, 1)], sem).start()
        return carry

    lax.fori_loop(0, tm, issue, 0, unroll=DMA_ISSUE_UNROLL)
    for _ in range(TOP_K):
        pltpu.make_async_copy(hp_ref, xs_ref.at[pl.ds(0, tm)], sem).wait()


def _dispatch(dest, pad_lo, pad_hi, hp, n_rows):
    t, half = hp.shape
    tm = _tile(t, 256)
    return pl.pallas_call(
        _dispatch_kernel,
        out_shape=jax.ShapeDtypeStruct((n_rows, half), U32),
        grid_spec=pltpu.PrefetchScalarGridSpec(
            num_scalar_prefetch=3,
            grid=(t // tm,),
            in_specs=[pl.BlockSpec((tm, half), lambda i, *_: (i, 0))],
            out_specs=pl.BlockSpec(memory_space=pl.ANY),
            scratch_shapes=[pltpu.VMEM((ZERO_CHUNK_ROWS, half), U32),
                            pltpu.SemaphoreType.DMA(()), pltpu.SemaphoreType.DMA(())],
        ),
        compiler_params=_params("arbitrary"),
        name="moe_dispatch",
    )(dest, pad_lo, pad_hi, hp)


def _expert_kernel(blk_expert_ref, next_expert_ref, n_used_ref, xs_ref, wu_hbm, bu_ref, wd_hbm, bd_ref, y_ref,
                   wu_f32, wd_f32, wu_bf, wd_bf, xb_ref, sems, *, layer):
    half = xs_ref.shape[1]
    f = wd_bf.shape[0]
    b = pl.program_id(0)
    live = b < n_used_ref[0]
    expert = blk_expert_ref[b]
    opens_expert = jnp.logical_or(b == 0, blk_expert_ref[jnp.maximum(b - 1, 0)] != expert)

    def weight_copies(e):
        return (pltpu.make_async_copy(wu_hbm.at[layer, e], wu_f32, sems.at[0]),
                pltpu.make_async_copy(wd_hbm.at[layer, e], wd_f32, sems.at[1]))

    @pl.when(jnp.logical_and(live, opens_expert))
    def _():
        @pl.when(b == 0)
        def _():
            for cp in weight_copies(expert):
                cp.start()

        for cp in weight_copies(expert):
            cp.wait()
        wu_bf[...] = wu_f32[...].astype(BF16)
        wd_bf[...] = wd_f32[...].astype(BF16)
        upcoming = next_expert_ref[b]

        @pl.when(upcoming >= 0)
        def _():
            for cp in weight_copies(upcoming):
                cp.start()

    @pl.when(live)
    def _():
        lo, hi = _unpack_halves(xs_ref[...])
        xb_ref[:, :half] = lo.astype(BF16)
        xb_ref[:, half:] = hi.astype(BF16)
        hu = jnp.dot(xb_ref[...], wu_bf[...], preferred_element_type=F32) + bu_ref[0]
        glu = jnp.minimum(hu[:, :f], SWIGLU_LIMIT)
        lin = jnp.clip(hu[:, f:], -SWIGLU_LIMIT, SWIGLU_LIMIT)
        act = (glu * jax.nn.sigmoid(SWIGLU_ALPHA * glu) * (lin + 1.0)).astype(BF16)
        cw = _tile(half, 512)
        for c in range(half // cw):
            pair = []
            for cols in (slice(c * cw, (c + 1) * cw), slice(half + c * cw, half + (c + 1) * cw)):
                y = jnp.dot(act, wd_bf[:, cols], preferred_element_type=F32) + bd_ref[0, :, cols]
                pair.append(lax.bitcast_convert_type(y.astype(BF16).astype(F32), U32))
            y_ref[:, c * cw:(c + 1) * cw] = (pair[0] >> 16) | (pair[1] & U32(HI_MASK))

    @pl.when(jnp.logical_not(live))
    def _():
        y_ref[...] = jnp.zeros_like(y_ref)


def _experts(blk_expert, next_expert, n_used, xs, w_up, b_up, w_down, b_down, *, layer, block_rows):
    n_rows, half = xs.shape
    _, n_experts, d, f2 = w_up.shape
    f = f2 // 2
    n_blocks = n_rows // block_rows

    def row_block(b, blk_expert, next_expert, n_used):
        return (jnp.minimum(b, n_used[0] - 1), 0)

    def expert_block(b, blk_expert, next_expert, n_used):
        return (blk_expert[b], 0, 0)

    return pl.pallas_call(
        functools.partial(_expert_kernel, layer=layer),
        out_shape=jax.ShapeDtypeStruct((n_rows, half), U32),
        grid_spec=pltpu.PrefetchScalarGridSpec(
            num_scalar_prefetch=3,
            grid=(n_blocks,),
            in_specs=[
                pl.BlockSpec((block_rows, half), row_block),
                pl.BlockSpec(memory_space=pl.ANY),
                pl.BlockSpec((1, 1, f2), expert_block),
                pl.BlockSpec(memory_space=pl.ANY),
                pl.BlockSpec((1, 1, d), expert_block),
            ],
            out_specs=pl.BlockSpec((block_rows, half), lambda b, *_: (b, 0)),
            scratch_shapes=[
                pltpu.VMEM((d, f2), F32), pltpu.VMEM((f, d), F32),
                pltpu.VMEM((d, f2), BF16), pltpu.VMEM((f, d), BF16),
                pltpu.VMEM((block_rows, d), BF16),
                pltpu.SemaphoreType.DMA((2,)),
            ],
        ),
        compiler_params=_params("arbitrary"),
        name="moe_experts",
    )(blk_expert, next_expert, n_used, xs, w_up, b_up.reshape(n_experts, 1, f2),
      w_down, b_down.reshape(n_experts, 1, d))


def _combine_kernel(dest_ref, yb_ref, x_ref, gates_ref, gt_ref, o_ref, buf_ref, sems):
    tm = x_ref.shape[0]
    half = buf_ref.shape[3]
    step = pl.program_id(0)
    slot = lax.rem(step, 2)

    def gather_tile(tile, into):
        def issue(t, carry):
            for k in range(TOP_K):
                d = dest_ref[(tile * tm + t) * TOP_K + k]
                pltpu.make_async_copy(yb_ref.at[pl.ds(d, 1)], buf_ref.at[into, k, pl.ds(t, 1)],
                                      sems.at[into]).start()
            return carry

        lax.fori_loop(0, tm, issue, 0, unroll=DMA_ISSUE_UNROLL)

    @pl.when(step == 0)
    def _():
        gather_tile(0, 0)

    @pl.when(step + 1 < pl.num_programs(0))
    def _():
        gather_tile(step + 1, 1 - slot)

    for k in range(TOP_K):
        pltpu.make_async_copy(yb_ref.at[pl.ds(0, tm)], buf_ref.at[slot, k], sems.at[slot]).wait()

    def weighted_sum(r, carry):
        rows = pl.ds(pl.multiple_of(r * SUBLANES, SUBLANES), SUBLANES)
        acc_lo = acc_hi = None
        for k in range(TOP_K):
            lo, hi = _unpack_halves(buf_ref[slot, k, rows, :])
            gate = gates_ref[rows, k:k + 1]
            acc_lo = gate * lo if acc_lo is None else acc_lo + gate * lo
            acc_hi = gate * hi if acc_hi is None else acc_hi + gate * hi
        o_ref[rows, :half] = x_ref[rows, :half] + gt_ref[0, :, :half] * acc_lo
        o_ref[rows, half:] = x_ref[rows, half:] + gt_ref[0, :, half:] * acc_hi
        return carry

    lax.fori_loop(0, tm // SUBLANES, weighted_sum, 0)


def _combine(dest, yb, x, gates, gt, *, seq_len):
    t, d = x.shape
    half = d // 2
    tm = _tile(seq_len, 256)
    per_seq = seq_len // tm
    return pl.pallas_call(
        _combine_kernel,
        out_shape=jax.ShapeDtypeStruct((t, d), F32),
        grid_spec=pltpu.PrefetchScalarGridSpec(
            num_scalar_prefetch=1,
            grid=(t // tm,),
            in_specs=[
                pl.BlockSpec(memory_space=pl.ANY),
                pl.BlockSpec((tm, d), lambda i, dest: (i, 0)),
                pl.BlockSpec((tm, LANES), lambda i, dest: (i, 0)),
                pl.BlockSpec((1, 1, d), lambda i, dest: (i // per_seq, 0, 0)),
            ],
            out_specs=pl.BlockSpec((tm, d), lambda i, dest: (i, 0)),
            scratch_shapes=[pltpu.VMEM((2, TOP_K, tm, half), U32), pltpu.SemaphoreType.DMA((2,))],
        ),
        input_output_aliases={2: 0},
        compiler_params=_params("arbitrary"),
        name="moe_combine",
    )(dest, yb, x, gates, gt)


def _moe_layer(x, g, sc, sh, gt, w_router, b_router, w_up, b_up, w_down, b_down, *, layer, seq_len, block_rows):
    t, d = x.shape
    n_experts = w_router.shape[1]
    assert 2 * n_experts <= LANES
    w_router_pad = jnp.pad(jnp.concatenate([w_router, w_router], axis=1), ((0, 0), (0, LANES - 2 * n_experts)))
    b_router_pad = jnp.pad(b_router, (0, LANES - n_experts)).reshape(1, LANES)
    hp, gates, eids, ranks, counts = _route(x, g, sc, sh, w_router_pad, b_router_pad,
                                            seq_len=seq_len, n_experts=n_experts)
    counts = counts[0, :n_experts]
    padded = (counts + block_rows - 1) // block_rows * block_rows
    seg_end = jnp.cumsum(padded)
    seg_start = seg_end - padded
    n_rows = t * TOP_K + n_experts * block_rows
    n_blocks = n_rows // block_rows
    expert_ids = jnp.arange(n_experts, dtype=I32)
    is_expert = eids[:, :TOP_K, None] == expert_ids
    dest = (ranks[:, :TOP_K] + jnp.sum(jnp.where(is_expert, seg_start, 0), axis=-1)).reshape(-1).astype(I32)
    blk_start = jnp.arange(n_blocks, dtype=I32) * block_rows
    blk_expert = jnp.minimum(jnp.sum(blk_start[:, None] >= seg_end[None, :], axis=1), n_experts - 1).astype(I32)
    n_used = (seg_end[-1:] // block_rows).astype(I32)
    later = lax.cummin(jnp.where(counts > 0, expert_ids, n_experts), axis=0, reverse=True)
    following = jnp.concatenate([later[1:], jnp.full((1,), n_experts, I32)])
    next_expert = jnp.where(following < n_experts, following, -1)[blk_expert].astype(I32)

    assert block_rows % ZERO_CHUNK_ROWS == 0
    pad_lo = jnp.concatenate([seg_start + counts, seg_end[-1:]]).astype(I32)
    pad_hi = jnp.concatenate([seg_end, jnp.full((1,), n_rows, I32)]).astype(I32)
    xs = _dispatch(dest, pad_lo, pad_hi, hp, n_rows)
    yb = _experts(blk_expert, next_expert, n_used, xs, w_up, b_up, w_down, b_down,
                  layer=layer, block_rows=block_rows)
    return _combine(dest, yb, x, gates, gt, seq_len=seq_len)


def _final_norm_kernel(x_ref, g_ref, o_ref):
    x = x_ref[...]
    o_ref[...] = x * lax.rsqrt(jnp.mean(x * x, axis=-1, keepdims=True) + NORM_EPS) * g_ref[...]


def _final_norm(x, g):
    t, d = x.shape
    tm = _tile(t, 512)
    return pl.pallas_call(
        _final_norm_kernel,
        out_shape=jax.ShapeDtypeStruct((t, d), F32),
        grid=(t // tm,),
        in_specs=[pl.BlockSpec((tm, d), lambda i: (i, 0)), pl.BlockSpec((1, d), lambda i: (0, 0))],
        out_specs=pl.BlockSpec((tm, d), lambda i: (i, 0)),
        compiler_params=_params("parallel"),
        name="final_norm",
    )(x, g)


def kernel(x, c, w_cond, b_cond, w_mod, b_mod, g_norm_mix, g_norm_ffn, pool_w_in, pool_w_group, pool_scale, pool_w_out, gmlp_w_in, gmlp_b_in, gmlp_ln_g, gmlp_ln_b, gmlp_w_s, gmlp_b_s, gmlp_w_out, moe_w_router, moe_b_router, moe_w_up, moe_b_up, moe_w_down, moe_b_down, g_final):
    bsz, seq_len, d = x.shape
    depth = w_mod.shape[0]
    mod = _modulation(c, w_cond, b_cond, w_mod, b_mod)
    xt = x.reshape(bsz * seq_len, d)
    zero_bias = jnp.zeros((1, d), F32)
    for layer in range(depth):
        sh1, sc1, gt1, sh2, sc2, gt2 = (mod[layer, :, m] for m in range(6))
        j = layer // 2
        g_mix = g_norm_mix[layer].reshape(1, d)
        if layer % 2 == 0:
            u = _norm_matmul(xt, g_mix, sc1, sh1, _bf16_weight(pool_w_in, j), zero_bias,
                             seq_len=seq_len, gelu=False)
            n_groups, gw, _ = pool_w_group.shape[1:]
            w_group = _bf16_weight(pool_w_group.reshape(-1, n_groups * gw, gw), j).reshape(n_groups, gw, gw)
            mid = _pool_middle(u, w_group, pool_scale[j].reshape(1, d), seq_len=seq_len)
            w_out = _bf16_weight(pool_w_out, j)
        else:
            uv = _norm_matmul(xt, g_mix, sc1, sh1, _bf16_weight(gmlp_w_in, j), gmlp_b_in[j].reshape(1, -1),
                              seq_len=seq_len, gelu=True)
            mid = _sgu_middle(uv, gmlp_ln_g[j].reshape(1, -1), gmlp_ln_b[j].reshape(1, -1),
                              gmlp_w_s[j], gmlp_b_s[j].T, seq_len=seq_len)
            w_out = _bf16_weight(gmlp_w_out, j)
        xt = _matmul_residual(mid, w_out, xt, gt1, seq_len=seq_len)
        xt = _moe_layer(xt, g_norm_ffn[layer].reshape(1, d), sc2, sh2, gt2,
                        moe_w_router[layer], moe_b_router[layer],
                        moe_w_up, moe_b_up[layer], moe_w_down, moe_b_down[layer],
                        layer=layer, seq_len=seq_len, block_rows=512)
    return _final_norm(xt, g_final.reshape(1, d)).reshape(bsz, seq_len, d)
```

```python
import functools

import jax
import jax.numpy as jnp
from jax import lax
from jax.experimental import pallas as pl
from jax.experimental.pallas import tpu as pltpu

F32 = jnp.float32
BF16 = jnp.bfloat16
U32 = jnp.uint32
I32 = jnp.int32
HIGHEST = lax.Precision.HIGHEST

NORM_EPS = 1e-5
POOL_WINDOWS = (2, 4, 8, 16)
TOP_K = 4
SWIGLU_LIMIT = 7.0
SWIGLU_ALPHA = 1.702

LANES = 128
SUBLANES = 8
MXU_COLS = 256
POOL_HALO = 16
ZERO_CHUNK_ROWS = 64
DMA_ISSUE_UNROLL = 8
VMEM_LIMIT_BYTES = 56 * 1024 * 1024
HI_MASK = 0xFFFF0000


def _tile(n, preferred):
    if n <= preferred:
        return n
    t = preferred - preferred % LANES
    while n % t:
        t -= LANES
    assert t > 0, (n, preferred)
    return t


def _params(*semantics):
    return pltpu.CompilerParams(dimension_semantics=semantics, vmem_limit_bytes=VMEM_LIMIT_BYTES)


def _modulated_rms_norm(x, g, sc, sh):
    y = x * lax.rsqrt(jnp.mean(x * x, axis=-1, keepdims=True) + NORM_EPS)
    return (y * g) * (1.0 + sc) + sh


def _pack_halves(v):
    half = v.shape[1] // 2
    bits = lax.bitcast_convert_type(v.astype(BF16).astype(F32), U32)
    return (bits[:, :half] >> 16) | (bits[:, half:] & U32(HI_MASK))


def _unpack_halves(w):
    lo = lax.bitcast_convert_type(w << 16, F32)
    hi = lax.bitcast_convert_type(w & U32(HI_MASK), F32)
    return lo, hi


def _cond_kernel(c_ref, w_ref, b_ref, o_ref):
    z = jnp.dot(c_ref[...], w_ref[...], precision=HIGHEST, preferred_element_type=F32) + b_ref[...]
    o_ref[...] = z * jax.nn.sigmoid(z)


def _mod_kernel(cond_ref, w_ref, b_ref, o_ref):
    o_ref[0] = jnp.dot(cond_ref[...], w_ref[0], precision=HIGHEST, preferred_element_type=F32) + b_ref[0]


def _modulation(c, w_cond, b_cond, w_mod, b_mod):
    bsz, d = c.shape
    rank = w_cond.shape[1]
    depth, _, n6 = w_mod.shape
    rows = -(-bsz // SUBLANES) * SUBLANES
    c_pad = jnp.pad(c, ((0, rows - bsz), (0, 0)))
    cond = pl.pallas_call(
        _cond_kernel,
        out_shape=jax.ShapeDtypeStruct((rows, rank), F32),
        compiler_params=pltpu.CompilerParams(vmem_limit_bytes=VMEM_LIMIT_BYTES),
        name="cond",
    )(c_pad, w_cond, b_cond.reshape(1, rank))
    tn = _tile(n6, 2048)
    mod = pl.pallas_call(
        _mod_kernel,
        out_shape=jax.ShapeDtypeStruct((depth, rows, n6), F32),
        grid=(depth, n6 // tn),
        in_specs=[
            pl.BlockSpec((rows, rank), lambda l, j: (0, 0)),
            pl.BlockSpec((1, rank, tn), lambda l, j: (l, 0, j)),
            pl.BlockSpec((1, 1, tn), lambda l, j: (l, 0, j)),
        ],
        out_specs=pl.BlockSpec((1, rows, tn), lambda l, j: (l, 0, j)),
        compiler_params=_params("parallel", "parallel"),
        name="mod",
    )(cond, w_mod, b_mod.reshape(depth, 1, n6))
    return mod[:, :bsz].reshape(depth, bsz, n6 // d, 1, d)


def _cast_kernel(w_ref, o_ref):
    o_ref[...] = w_ref[0].astype(o_ref.dtype)


def _bf16_weight(w, index):
    _, rows, cols = w.shape
    tr = _tile(rows, 512)
    return pl.pallas_call(
        _cast_kernel,
        out_shape=jax.ShapeDtypeStruct((rows, cols), BF16),
        grid=(rows // tr,),
        in_specs=[pl.BlockSpec((1, tr, cols), lambda i: (index, i, 0))],
        out_specs=pl.BlockSpec((tr, cols), lambda i: (i, 0)),
        compiler_params=_params("parallel"),
        name="weight_to_bf16",
    )(w)


def _norm_mm_kernel(x_ref, g_ref, sc_ref, sh_ref, w_ref, b_ref, o_ref, h_ref, *, gelu):
    @pl.when(pl.program_id(1) == 0)
    def _():
        h_ref[...] = _modulated_rms_norm(x_ref[...], g_ref[...], sc_ref[0], sh_ref[0]).astype(BF16)

    acc = jnp.dot(h_ref[...], w_ref[...], preferred_element_type=F32) + b_ref[...]
    if gelu:
        acc = 0.5 * acc * (1.0 + lax.erf(acc * F32(0.7071067811865476)))
    o_ref[...] = acc.astype(o_ref.dtype)


def _norm_matmul(x, g, sc, sh, w, b, *, seq_len, gelu):
    t, d = x.shape
    n = w.shape[1]
    tm = _tile(seq_len, 512)
    tn = _tile(n, 1024)
    per_seq = seq_len // tm
    return pl.pallas_call(
        functools.partial(_norm_mm_kernel, gelu=gelu),
        out_shape=jax.ShapeDtypeStruct((t, n), BF16),
        grid=(t // tm, n // tn),
        in_specs=[
            pl.BlockSpec((tm, d), lambda i, j: (i, 0)),
            pl.BlockSpec((1, d), lambda i, j: (0, 0)),
            pl.BlockSpec((1, 1, d), lambda i, j: (i // per_seq, 0, 0)),
            pl.BlockSpec((1, 1, d), lambda i, j: (i // per_seq, 0, 0)),
            pl.BlockSpec((d, tn), lambda i, j: (0, j)),
            pl.BlockSpec((1, tn), lambda i, j: (0, j)),
        ],
        out_specs=pl.BlockSpec((tm, tn), lambda i, j: (i, j)),
        scratch_shapes=[pltpu.VMEM((tm, d), BF16)],
        compiler_params=_params("parallel", "arbitrary"),
        name="norm_matmul_gelu" if gelu else "norm_matmul",
    )(x, g, sc, sh, w, b)


def _mm_residual_kernel(a_ref, w_ref, x_ref, gate_ref, o_ref):
    y = jnp.dot(a_ref[...], w_ref[...], preferred_element_type=F32)
    o_ref[...] = x_ref[...] + gate_ref[0] * y


def _matmul_residual(a, w, x, gate, *, seq_len):
    t, k = a.shape
    d = w.shape[1]
    tm = _tile(seq_len, 1024)
    tn = _tile(d, 512)
    per_seq = seq_len // tm
    return pl.pallas_call(
        _mm_residual_kernel,
        out_shape=jax.ShapeDtypeStruct((t, d), F32),
        grid=(t // tm, d // tn),
        in_specs=[
            pl.BlockSpec((tm, k), lambda i, j: (i, 0)),
            pl.BlockSpec((k, tn), lambda i, j: (0, j)),
            pl.BlockSpec((tm, tn), lambda i, j: (i, j)),
            pl.BlockSpec((1, 1, tn), lambda i, j: (i // per_seq, 0, j)),
        ],
        out_specs=pl.BlockSpec((tm, tn), lambda i, j: (i, j)),
        compiler_params=_params("parallel", "parallel"),
        name="matmul_residual",
    )(a, w, x, gate)


def _pool_kernel(u_ref, halo_ref, wg_ref, scale_ref, o_ref, ext_ref, *, seq_len):
    tm = u_ref.shape[0]
    gw = wg_ref.shape[1]
    start = lax.rem(pl.program_id(0) * tm, seq_len)
    ext_ref[0:POOL_HALO] = jnp.where(start == 0, 0.0, halo_ref[...].astype(F32))
    ext_ref[POOL_HALO:] = u_ref[...].astype(F32)
    pos = start + lax.broadcasted_iota(I32, (tm, 1), 0)
    for g, window in enumerate(POOL_WINDOWS):
        cols = slice(g * gw, (g + 1) * gw)
        cur = ext_ref[POOL_HALO:POOL_HALO + tm, cols]
        total = cur
        for lag in range(1, window):
            total = total + ext_ref[POOL_HALO - lag:POOL_HALO - lag + tm, cols]
        inv_cnt = 1.0 / jnp.minimum(pos + 1, window).astype(F32)
        pooled = total * inv_cnt - cur
        mixed = jnp.dot(pooled.astype(BF16), wg_ref[g], preferred_element_type=F32)
        o_ref[:, cols] = (mixed * scale_ref[:, cols]).astype(o_ref.dtype)


def _pool_middle(u, w_group, scale, *, seq_len):
    t, d = u.shape
    assert w_group.shape[0] == len(POOL_WINDOWS) and max(POOL_WINDOWS) <= POOL_HALO
    tm = _tile(seq_len, 512)
    halo_blocks = tm // POOL_HALO
    return pl.pallas_call(
        functools.partial(_pool_kernel, seq_len=seq_len),
        out_shape=jax.ShapeDtypeStruct((t, d), BF16),
        grid=(t // tm,),
        in_specs=[
            pl.BlockSpec((tm, d), lambda i: (i, 0)),
            pl.BlockSpec((POOL_HALO, d), lambda i: (jnp.maximum(i * halo_blocks - 1, 0), 0)),
            pl.BlockSpec(w_group.shape, lambda i: (0, 0, 0)),
            pl.BlockSpec((1, d), lambda i: (0, 0)),
        ],
        out_specs=pl.BlockSpec((tm, d), lambda i: (i, 0)),
        scratch_shapes=[pltpu.VMEM((tm + POOL_HALO, d), F32)],
        compiler_params=_params("parallel"),
        name="pool_middle",
    )(u, u, w_group, scale)


def _sgu_kernel(u_ref, v_ref, lng_ref, lnb_ref, ws_ref, bst_ref, o_ref, vn_ref):
    tm, width = u_ref.shape
    n_heads, chunk, _ = ws_ref.shape
    hw = width // n_heads
    v = v_ref[...].astype(F32)
    vc = v - jnp.mean(v, axis=-1, keepdims=True)
    vn = vc * lax.rsqrt(jnp.mean(vc * vc, axis=-1, keepdims=True) + NORM_EPS)
    vn_ref[...] = (vn * lng_ref[...] + lnb_ref[...]).astype(BF16)
    causal = (lax.broadcasted_iota(I32, (chunk, chunk), 0) >= lax.broadcasted_iota(I32, (chunk, chunk), 1))
    for h in range(n_heads):
        w_causal = jnp.where(causal, ws_ref[h], 0.0).astype(BF16)
        bias = bst_ref[:, h:h + 1]
        cols = slice(h * hw, (h + 1) * hw)
        for c in range(tm // chunk):
            rows = slice(c * chunk, (c + 1) * chunk)
            mixed = jnp.dot(w_causal, vn_ref[rows, cols], preferred_element_type=F32) + bias
            o_ref[rows, cols] = (u_ref[rows, cols].astype(F32) * mixed).astype(o_ref.dtype)


def _sgu_middle(uv, ln_g, ln_b, w_s, b_s_t, *, seq_len):
    t, two_w = uv.shape
    width = two_w // 2
    chunk = w_s.shape[1]
    tm = _tile(seq_len, 4 * chunk)
    return pl.pallas_call(
        _sgu_kernel,
        out_shape=jax.ShapeDtypeStruct((t, width), BF16),
        grid=(t // tm,),
        in_specs=[
            pl.BlockSpec((tm, width), lambda i: (i, 0)),
            pl.BlockSpec((tm, width), lambda i: (i, 1)),
            pl.BlockSpec((1, width), lambda i: (0, 0)),
            pl.BlockSpec((1, width), lambda i: (0, 0)),
            pl.BlockSpec(w_s.shape, lambda i: (0, 0, 0)),
            pl.BlockSpec(b_s_t.shape, lambda i: (0, 0)),
        ],
        out_specs=pl.BlockSpec((tm, width), lambda i: (i, 0)),
        scratch_shapes=[pltpu.VMEM((tm, width), BF16)],
        compiler_params=_params("parallel"),
        name="sgu_middle",
    )(uv, uv, ln_g, ln_b, w_s, b_s_t)


def _router_kernel(x_ref, g_ref, sc_ref, sh_ref, wr_ref, br_ref,
                   hp_ref, gates_ref, eid_ref, rank_ref, counts_ref, carry_ref, *, n_experts):
    tm = x_ref.shape[0]

    @pl.when(pl.program_id(0) == 0)
    def _():
        carry_ref[...] = jnp.zeros_like(carry_ref)

    h = _modulated_rms_norm(x_ref[...], g_ref[...], sc_ref[0], sh_ref[0])
    hp_ref[...] = _pack_halves(h)

    lane = lax.broadcasted_iota(I32, (tm, LANES), 1)
    w_lane = lax.broadcasted_iota(I32, wr_ref.shape, 1)
    w = wr_ref[...]
    w_hi = w.astype(BF16)
    w_lo = (w - w_hi.astype(F32)).astype(BF16)
    h_hi = h.astype(BF16)
    h_lo = (h - h_hi.astype(F32)).astype(BF16)
    first = jnp.dot(h_hi, jnp.where(w_lane < n_experts, w_hi, w_lo), preferred_element_type=F32)
    second = jnp.dot(h_lo, w_hi, preferred_element_type=F32)
    logits = first + second + pltpu.roll(first, LANES - n_experts, 1) + br_ref[...]
    remaining = jnp.where(lane < n_experts, logits, -jnp.inf)
    selected = jnp.zeros((tm, LANES), F32)
    top_vals, top_ids = [], []
    for _ in range(TOP_K):
        best = jnp.max(remaining, axis=-1, keepdims=True)
        idx = jnp.min(jnp.where(remaining == best, lane, LANES), axis=-1, keepdims=True)
        hit = lane == idx
        selected = jnp.where(hit, 1.0, selected)
        remaining = jnp.where(hit, -jnp.inf, remaining)
        top_vals.append(best)
        top_ids.append(idx)
    exps = [jnp.exp(v - top_vals[0]) for v in top_vals]
    denom = exps[0]
    for e in exps[1:]:
        denom = denom + e

    lower = (lax.broadcasted_iota(I32, (tm, tm), 0) >= lax.broadcasted_iota(I32, (tm, tm), 1))
    incl = jnp.dot(lower.astype(BF16), selected.astype(BF16), preferred_element_type=F32)
    before = incl - selected + carry_ref[...]
    gates = jnp.zeros((tm, LANES), F32)
    eids = jnp.zeros((tm, LANES), I32)
    ranks = jnp.zeros((tm, LANES), I32)
    for k in range(TOP_K):
        rank_k = jnp.sum(jnp.where(lane == top_ids[k], before, 0.0), axis=-1, keepdims=True)
        gates = jnp.where(lane == k, exps[k] / denom, gates)
        eids = jnp.where(lane == k, top_ids[k], eids)
        ranks = jnp.where(lane == k, rank_k.astype(I32), ranks)
    gates_ref[...] = gates
    eid_ref[...] = eids
    rank_ref[...] = ranks
    carry_ref[...] = carry_ref[...] + incl[tm - 1:tm, :]
    counts_ref[...] = carry_ref[...].astype(I32)


def _route(x, g, sc, sh, w_router_pad, b_router_pad, *, seq_len, n_experts):
    t, d = x.shape
    tm = _tile(seq_len, 512)
    per_seq = seq_len // tm
    tok_spec = pl.BlockSpec((tm, LANES), lambda i: (i, 0))
    return pl.pallas_call(
        functools.partial(_router_kernel, n_experts=n_experts),
        out_shape=[
            jax.ShapeDtypeStruct((t, d // 2), U32),
            jax.ShapeDtypeStruct((t, LANES), F32),
            jax.ShapeDtypeStruct((t, LANES), I32),
            jax.ShapeDtypeStruct((t, LANES), I32),
            jax.ShapeDtypeStruct((1, LANES), I32),
        ],
        grid=(t // tm,),
        in_specs=[
            pl.BlockSpec((tm, d), lambda i: (i, 0)),
            pl.BlockSpec((1, d), lambda i: (0, 0)),
            pl.BlockSpec((1, 1, d), lambda i: (i // per_seq, 0, 0)),
            pl.BlockSpec((1, 1, d), lambda i: (i // per_seq, 0, 0)),
            pl.BlockSpec((d, LANES), lambda i: (0, 0)),
            pl.BlockSpec((1, LANES), lambda i: (0, 0)),
        ],
        out_specs=[
            pl.BlockSpec((tm, d // 2), lambda i: (i, 0)),
            tok_spec, tok_spec, tok_spec,
            pl.BlockSpec((1, LANES), lambda i: (0, 0)),
        ],
        scratch_shapes=[pltpu.VMEM((1, LANES), F32)],
        compiler_params=_params("arbitrary"),
        name="moe_router",
    )(x, g, sc, sh, w_router_pad, b_router_pad)


def _dispatch_kernel(dest_ref, pad_lo_ref, pad_hi_ref, hp_ref, xs_ref, zero_ref, sem, pad_sem):
    tm = hp_ref.shape[0]
    base = pl.program_id(0) * tm

    @pl.when(pl.program_id(0) == 0)
    def _():
        zero_ref[...] = jnp.zeros_like(zero_ref)
        chunk = zero_ref.shape[0]

        def chunk_copy(hi, i):
            start = pl.multiple_of(hi - (i + 1) * chunk, chunk)
            return pltpu.make_async_copy(zero_ref, xs_ref.at[pl.ds(start, chunk)], pad_sem)

        def row_copy(r):
            return pltpu.make_async_copy(zero_ref.at[pl.ds(0, 1)], xs_ref.at[pl.ds(r, 1)], pad_sem)

        def for_each_copy(act):
            for e in range(pad_lo_ref.shape[0]):
                lo, hi = pad_lo_ref[e], pad_hi_ref[e]
                n_chunks = (hi - lo) // chunk
                lax.fori_loop(0, n_chunks, lambda i, c: (act(chunk_copy(hi, i)), c)[1], 0)
                lax.fori_loop(lo, hi - n_chunks * chunk, lambda r, c: (act(row_copy(r)), c)[1], 0)

        for_each_copy(lambda cp: cp.start())
        for_each_copy(lambda cp: cp.wait())

    def issue(t, carry):
        for k in range(TOP_K):
            d = dest_ref[(base + t) * TOP_K + k]
            pltpu.make_async_copy(hp_ref.at[pl.ds(t, 1)], xs_ref.at[pl.ds(d, 1)], sem).start()
        return carry

    lax.fori_loop(0, tm, issue, 0, unroll=DMA_ISSUE_UNROLL)
    for _ in range(TOP_K):
        pltpu.make_async_copy(hp_ref, xs_ref.at[pl.ds(0, tm)], sem).wait()


def _dispatch(dest, pad_lo, pad_hi, hp, n_rows):
    t, half = hp.shape
    tm = _tile(t, 256)
    return pl.pallas_call(
        _dispatch_kernel,
        out_shape=jax.ShapeDtypeStruct((n_rows, half), U32),
        grid_spec=pltpu.PrefetchScalarGridSpec(
            num_scalar_prefetch=3,
            grid=(t // tm,),
            in_specs=[pl.BlockSpec((tm, half), lambda i, *_: (i, 0))],
            out_specs=pl.BlockSpec(memory_space=pl.ANY),
            scratch_shapes=[pltpu.VMEM((ZERO_CHUNK_ROWS, half), U32),
                            pltpu.SemaphoreType.DMA(()), pltpu.SemaphoreType.DMA(())],
        ),
        compiler_params=_params("arbitrary"),
        name="moe_dispatch",
    )(dest, pad_lo, pad_hi, hp)


def _expert_kernel(blk_expert_ref, next_expert_ref, n_used_ref, xs_ref, wu_hbm, bu_ref, wd_hbm, bd_ref, y_ref,
                   wu_f32, wd_f32, wu_bf, wd_bf, xb_ref, sems, *, layer):
    half = xs_ref.shape[1]
    f = wd_bf.shape[0]
    b = pl.program_id(0)
    live = b < n_used_ref[0]
    expert = blk_expert_ref[b]
    opens_expert = jnp.logical_or(b == 0, blk_expert_ref[jnp.maximum(b - 1, 0)] != expert)

    def weight_copies(e):
        return (pltpu.make_async_copy(wu_hbm.at[layer, e], wu_f32, sems.at[0]),
                pltpu.make_async_copy(wd_hbm.at[layer, e], wd_f32, sems.at[1]))

    @pl.when(jnp.logical_and(live, opens_expert))
    def _():
        @pl.when(b == 0)
        def _():
            for cp in weight_copies(expert):
                cp.start()

        for cp in weight_copies(expert):
            cp.wait()
        wu_bf[...] = wu_f32[...].astype(BF16)
        wd_bf[...] = wd_f32[...].astype(BF16)
        upcoming = next_expert_ref[b]

        @pl.when(upcoming >= 0)
        def _():
            for cp in weight_copies(upcoming):
                cp.start()

    @pl.when(live)
    def _():
        lo, hi = _unpack_halves(xs_ref[...])
        xb_ref[:, :half] = lo.astype(BF16)
        xb_ref[:, half:] = hi.astype(BF16)
        rows_all = xs_ref.shape[0]
        sub = rows_all // 2
        wide = 2 * MXU_COLS
        row_halves = [slice(s * sub, (s + 1) * sub) for s in range(2)]
        wide = min(wide, wu_bf.shape[1])
        hu_parts = [[jnp.dot(xb_ref[rows, :], wu_bf[:, :wide], preferred_element_type=F32)] for rows in row_halves]
        if wide < wu_bf.shape[1]:
            for s, rows in enumerate(row_halves):
                hu_parts[s].append(jnp.dot(xb_ref[rows, :], wu_bf[:, wide:], preferred_element_type=F32))
        for s, rows in enumerate(row_halves):
            hu = jnp.concatenate(hu_parts[s], axis=1) + bu_ref[0]
            glu = jnp.minimum(hu[:, :f], SWIGLU_LIMIT)
            lin = jnp.clip(hu[:, f:], -SWIGLU_LIMIT, SWIGLU_LIMIT)
            act = (glu * jax.nn.sigmoid(SWIGLU_ALPHA * glu) * (lin + 1.0)).astype(BF16)
            cw = _tile(half, 512)
            for c in range(half // cw):
                pair = []
                for cols in (slice(c * cw, (c + 1) * cw), slice(half + c * cw, half + (c + 1) * cw)):
                    y = jnp.dot(act, wd_bf[:, cols], preferred_element_type=F32) + bd_ref[0, :, cols]
                    pair.append(lax.bitcast_convert_type(y.astype(BF16).astype(F32), U32))
                y_ref[rows, c * cw:(c + 1) * cw] = (pair[0] >> 16) | (pair[1] & U32(HI_MASK))

    @pl.when(jnp.logical_not(live))
    def _():
        y_ref[...] = jnp.zeros_like(y_ref)


def _experts(blk_expert, next_expert, n_used, xs, w_up, b_up, w_down, b_down, *, layer, block_rows):
    n_rows, half = xs.shape
    _, n_experts, d, f2 = w_up.shape
    f = f2 // 2
    n_blocks = n_rows // block_rows

    def row_block(b, blk_expert, next_expert, n_used):
        return (jnp.minimum(b, n_used[0] - 1), 0)

    def expert_block(b, blk_expert, next_expert, n_used):
        return (blk_expert[b], 0, 0)

    return pl.pallas_call(
        functools.partial(_expert_kernel, layer=layer),
        out_shape=jax.ShapeDtypeStruct((n_rows, half), U32),
        grid_spec=pltpu.PrefetchScalarGridSpec(
            num_scalar_prefetch=3,
            grid=(n_blocks,),
            in_specs=[
                pl.BlockSpec((block_rows, half), row_block),
                pl.BlockSpec(memory_space=pl.ANY),
                pl.BlockSpec((1, 1, f2), expert_block),
                pl.BlockSpec(memory_space=pl.ANY),
                pl.BlockSpec((1, 1, d), expert_block),
            ],
            out_specs=pl.BlockSpec((block_rows, half), lambda b, *_: (b, 0)),
            scratch_shapes=[
                pltpu.VMEM((d, f2), F32), pltpu.VMEM((f, d), F32),
                pltpu.VMEM((d, f2), BF16), pltpu.VMEM((f, d), BF16),
                pltpu.VMEM((block_rows, d), BF16),
                pltpu.SemaphoreType.DMA((2,)),
            ],
        ),
        compiler_params=_params("arbitrary"),
        name="moe_experts",
    )(blk_expert, next_expert, n_used, xs, w_up, b_up.reshape(n_experts, 1, f2),
      w_down, b_down.reshape(n_experts, 1, d))


def _combine_kernel(dest_ref, yb_ref, x_ref, gates_ref, gt_ref, gf_ref, o_ref, buf_ref, sems, *, final_norm):
    tm = x_ref.shape[0]
    half = buf_ref.shape[3]
    step = pl.program_id(0)
    slot = lax.rem(step, 2)

    def request_rows(tile, into, group):
        for j in range(SUBLANES):
            t = group * SUBLANES + j
            for k in range(TOP_K):
                d = dest_ref[(tile * tm + t) * TOP_K + k]
                pltpu.make_async_copy(yb_ref.at[pl.ds(d, 1)], buf_ref.at[into, k, pl.ds(t, 1)],
                                      sems.at[into]).start()

    def weighted_sum(group):
        rows = pl.ds(pl.multiple_of(group * SUBLANES, SUBLANES), SUBLANES)
        acc_lo = acc_hi = None
        for k in range(TOP_K):
            lo, hi = _unpack_halves(buf_ref[slot, k, rows, :])
            gate = gates_ref[rows, k:k + 1]
            acc_lo = gate * lo if acc_lo is None else acc_lo + gate * lo
            acc_hi = gate * hi if acc_hi is None else acc_hi + gate * hi
        out_lo = x_ref[rows, :half] + gt_ref[0, :, :half] * acc_lo
        out_hi = x_ref[rows, half:] + gt_ref[0, :, half:] * acc_hi
        if final_norm:
            ms = (jnp.sum(out_lo * out_lo, axis=-1, keepdims=True)
                  + jnp.sum(out_hi * out_hi, axis=-1, keepdims=True)) / (2 * half)
            scale = lax.rsqrt(ms + NORM_EPS)
            out_lo = out_lo * scale * gf_ref[:, :half]
            out_hi = out_hi * scale * gf_ref[:, half:]
        o_ref[rows, :half] = out_lo
        o_ref[rows, half:] = out_hi

    n_groups = tm // SUBLANES

    @pl.when(step == 0)
    def _():
        lax.fori_loop(0, n_groups, lambda g, c: (request_rows(0, 0, g), c)[1], 0)

    for k in range(TOP_K):
        pltpu.make_async_copy(yb_ref.at[pl.ds(0, tm)], buf_ref.at[slot, k], sems.at[slot]).wait()

    has_next = step + 1 < pl.num_programs(0)

    @pl.when(has_next)
    def _():
        def body(g, c):
            request_rows(step + 1, 1 - slot, g)
            weighted_sum(g)
            return c

        lax.fori_loop(0, n_groups, body, 0)

    @pl.when(jnp.logical_not(has_next))
    def _():
        lax.fori_loop(0, n_groups, lambda g, c: (weighted_sum(g), c)[1], 0)


def _combine(dest, yb, x, gates, gt, g_final, *, seq_len, final_norm):
    t, d = x.shape
    half = d // 2
    tm = _tile(seq_len, 256)
    per_seq = seq_len // tm
    return pl.pallas_call(
        functools.partial(_combine_kernel, final_norm=final_norm),
        out_shape=jax.ShapeDtypeStruct((t, d), F32),
        grid_spec=pltpu.PrefetchScalarGridSpec(
            num_scalar_prefetch=1,
            grid=(t // tm,),
            in_specs=[
                pl.BlockSpec(memory_space=pl.ANY),
                pl.BlockSpec((tm, d), lambda i, dest: (i, 0)),
                pl.BlockSpec((tm, LANES), lambda i, dest: (i, 0)),
                pl.BlockSpec((1, 1, d), lambda i, dest: (i // per_seq, 0, 0)),
                pl.BlockSpec((1, d), lambda i, dest: (0, 0)),
            ],
            out_specs=pl.BlockSpec((tm, d), lambda i, dest: (i, 0)),
            scratch_shapes=[pltpu.VMEM((2, TOP_K, tm, half), U32), pltpu.SemaphoreType.DMA((2,))],
        ),
        input_output_aliases={2: 0},
        compiler_params=_params("arbitrary"),
        name="moe_combine",
    )(dest, yb, x, gates, gt, g_final)


def _moe_layer(x, g, sc, sh, gt, w_router, b_router, w_up, b_up, w_down, b_down, g_final,
               *, layer, seq_len, block_rows, final_norm):
    t, d = x.shape
    n_experts = w_router.shape[1]
    assert 2 * n_experts <= LANES
    w_router_pad = jnp.pad(jnp.concatenate([w_router, w_router], axis=1), ((0, 0), (0, LANES - 2 * n_experts)))
    b_router_pad = jnp.pad(b_router, (0, LANES - n_experts)).reshape(1, LANES)
    hp, gates, eids, ranks, counts = _route(x, g, sc, sh, w_router_pad, b_router_pad,
                                            seq_len=seq_len, n_experts=n_experts)
    counts = counts[0, :n_experts]
    padded = (counts + block_rows - 1) // block_rows * block_rows
    seg_end = jnp.cumsum(padded)
    seg_start = seg_end - padded
    n_rows = t * TOP_K + n_experts * block_rows
    n_blocks = n_rows // block_rows
    expert_ids = jnp.arange(n_experts, dtype=I32)
    is_expert = eids[:, :TOP_K, None] == expert_ids
    dest = (ranks[:, :TOP_K] + jnp.sum(jnp.where(is_expert, seg_start, 0), axis=-1)).reshape(-1).astype(I32)
    blk_start = jnp.arange(n_blocks, dtype=I32) * block_rows
    blk_expert = jnp.minimum(jnp.sum(blk_start[:, None] >= seg_end[None, :], axis=1), n_experts - 1).astype(I32)
    n_used = (seg_end[-1:] // block_rows).astype(I32)
    later = lax.cummin(jnp.where(counts > 0, expert_ids, n_experts), axis=0, reverse=True)
    following = jnp.concatenate([later[1:], jnp.full((1,), n_experts, I32)])
    next_expert = jnp.where(following < n_experts, following, -1)[blk_expert].astype(I32)

    assert block_rows % ZERO_CHUNK_ROWS == 0
    pad_lo = jnp.concatenate([seg_start + counts, seg_end[-1:]]).astype(I32)
    pad_hi = jnp.concatenate([seg_end, jnp.full((1,), n_rows, I32)]).astype(I32)
    xs = _dispatch(dest, pad_lo, pad_hi, hp, n_rows)
    yb = _experts(blk_expert, next_expert, n_used, xs, w_up, b_up, w_down, b_down,
                  layer=layer, block_rows=block_rows)
    return _combine(dest, yb, x, gates, gt, g_final, seq_len=seq_len, final_norm=final_norm)


def kernel(x, c, w_cond, b_cond, w_mod, b_mod, g_norm_mix, g_norm_ffn, pool_w_in, pool_w_group, pool_scale, pool_w_out, gmlp_w_in, gmlp_b_in, gmlp_ln_g, gmlp_ln_b, gmlp_w_s, gmlp_b_s, gmlp_w_out, moe_w_router, moe_b_router, moe_w_up, moe_b_up, moe_w_down, moe_b_down, g_final):
    bsz, seq_len, d = x.shape
    depth = w_mod.shape[0]
    mod = _modulation(c, w_cond, b_cond, w_mod, b_mod)
    xt = x.reshape(bsz * seq_len, d)
    zero_bias = jnp.zeros((1, d), F32)
    for layer in range(depth):
        sh1, sc1, gt1, sh2, sc2, gt2 = (mod[layer, :, m] for m in range(6))
        j = layer // 2
        g_mix = g_norm_mix[layer].reshape(1, d)
        if layer % 2 == 0:
            u = _norm_matmul(xt, g_mix, sc1, sh1, _bf16_weight(pool_w_in, j), zero_bias,
                             seq_len=seq_len, gelu=False)
            n_groups, gw, _ = pool_w_group.shape[1:]
            w_group = _bf16_weight(pool_w_group.reshape(-1, n_groups * gw, gw), j).reshape(n_groups, gw, gw)
            mid = _pool_middle(u, w_group, pool_scale[j].reshape(1, d), seq_len=seq_len)
            w_out = _bf16_weight(pool_w_out, j)
        else:
            uv = _norm_matmul(xt, g_mix, sc1, sh1, _bf16_weight(gmlp_w_in, j), gmlp_b_in[j].reshape(1, -1),
                              seq_len=seq_len, gelu=True)
            mid = _sgu_middle(uv, gmlp_ln_g[j].reshape(1, -1), gmlp_ln_b[j].reshape(1, -1),
                              gmlp_w_s[j], gmlp_b_s[j].T, seq_len=seq_len)
            w_out = _bf16_weight(gmlp_w_out, j)
        xt = _matmul_residual(mid, w_out, xt, gt1, seq_len=seq_len)
        xt = _moe_layer(xt, g_norm_ffn[layer].reshape(1, d), sc2, sh2, gt2,
                        moe_w_router[layer], moe_b_router[layer],
                        moe_w_up, moe_b_up[layer], moe_w_down, moe_b_down[layer], g_final.reshape(1, d),
                        layer=layer, seq_len=seq_len, block_rows=512, final_norm=layer == depth - 1)
    return xt.reshape(bsz, seq_len, d)
```

```python
import functools

import jax
import jax.numpy as jnp
from jax import lax
from jax.experimental import pallas as pl
from jax.experimental.pallas import tpu as pltpu

F32 = jnp.float32
BF16 = jnp.bfloat16
U32 = jnp.uint32
I32 = jnp.int32
HIGHEST = lax.Precision.HIGHEST

NORM_EPS = 1e-5
POOL_WINDOWS = (2, 4, 8, 16)
TOP_K = 4
SWIGLU_LIMIT = 7.0
SWIGLU_ALPHA = 1.702

LANES = 128
SUBLANES = 8
MXU_COLS = 256
POOL_HALO = 16
NORM_GROUP_ROWS = 16
ZERO_CHUNK_ROWS = 64
DMA_ISSUE_UNROLL = 8
VMEM_LIMIT_BYTES = 56 * 1024 * 1024
HI_MASK = 0xFFFF0000


def _tile(n, preferred):
    if n <= preferred:
        return n
    t = preferred - preferred % LANES
    while n % t:
        t -= LANES
    assert t > 0, (n, preferred)
    return t


def _params(*semantics):
    return pltpu.CompilerParams(dimension_semantics=semantics, vmem_limit_bytes=VMEM_LIMIT_BYTES)


def _sublane_rows(v):
    return jnp.broadcast_to(v, (SUBLANES, v.shape[1]))


def _affine_rows(y, scale_rows, shift_rows):
    parts = []
    for i in range(y.shape[0] // SUBLANES):
        part = y[i * SUBLANES:(i + 1) * SUBLANES] * scale_rows
        parts.append(part if shift_rows is None else part + shift_rows)
    return jnp.concatenate(parts, axis=0)


def _rms_normalise(x):
    return x * lax.rsqrt(jnp.mean(x * x, axis=-1, keepdims=True) + NORM_EPS)


def _set_modulation(mod_ref, g_ref, sc_ref, sh_ref):
    mod_ref[0] = _sublane_rows(g_ref[...] * (1.0 + sc_ref[0]))
    mod_ref[1] = _sublane_rows(sh_ref[0])


def _modulated_rms_norm(x, mod_ref):
    return _affine_rows(_rms_normalise(x), mod_ref[0], mod_ref[1])


def _for_each_row_group(x_ref, body):
    n_groups = x_ref.shape[0] // NORM_GROUP_ROWS

    def step(r, carry):
        body(pl.ds(pl.multiple_of(r * NORM_GROUP_ROWS, NORM_GROUP_ROWS), NORM_GROUP_ROWS))
        return carry

    lax.fori_loop(0, n_groups, step, 0, unroll=2)


def _pack_halves(v):
    half = v.shape[1] // 2
    bits = lax.bitcast_convert_type(v.astype(BF16).astype(F32), U32)
    return (bits[:, :half] >> 16) | (bits[:, half:] & U32(HI_MASK))


def _unpack_halves(w):
    lo = lax.bitcast_convert_type(w << 16, F32)
    hi = lax.bitcast_convert_type(w & U32(HI_MASK), F32)
    return lo, hi


def _cond_kernel(c_ref, w_ref, b_ref, o_ref):
    z = jnp.dot(c_ref[...], w_ref[...], precision=HIGHEST, preferred_element_type=F32) + b_ref[...]
    o_ref[...] = z * jax.nn.sigmoid(z)


def _mod_kernel(cond_ref, w_ref, b_ref, o_ref):
    o_ref[0] = jnp.dot(cond_ref[...], w_ref[0], precision=HIGHEST, preferred_element_type=F32) + b_ref[0]


def _modulation(c, w_cond, b_cond, w_mod, b_mod):
    bsz, d = c.shape
    rank = w_cond.shape[1]
    depth, _, n6 = w_mod.shape
    rows = -(-bsz // SUBLANES) * SUBLANES
    c_pad = jnp.pad(c, ((0, rows - bsz), (0, 0)))
    cond = pl.pallas_call(
        _cond_kernel,
        out_shape=jax.ShapeDtypeStruct((rows, rank), F32),
        compiler_params=pltpu.CompilerParams(vmem_limit_bytes=VMEM_LIMIT_BYTES),
        name="cond",
    )(c_pad, w_cond, b_cond.reshape(1, rank))
    tn = _tile(n6, 2048)
    mod = pl.pallas_call(
        _mod_kernel,
        out_shape=jax.ShapeDtypeStruct((depth, rows, n6), F32),
        grid=(depth, n6 // tn),
        in_specs=[
            pl.BlockSpec((rows, rank), lambda l, j: (0, 0)),
            pl.BlockSpec((1, rank, tn), lambda l, j: (l, 0, j)),
            pl.BlockSpec((1, 1, tn), lambda l, j: (l, 0, j)),
        ],
        out_specs=pl.BlockSpec((1, rows, tn), lambda l, j: (l, 0, j)),
        compiler_params=_params("parallel", "parallel"),
        name="mod",
    )(cond, w_mod, b_mod.reshape(depth, 1, n6))
    return mod[:, :bsz].reshape(depth, bsz, n6 // d, 1, d)


def _cast_kernel(w_ref, o_ref):
    o_ref[...] = w_ref[0].astype(o_ref.dtype)


def _bf16_weight(w, index):
    _, rows, cols = w.shape
    tr = _tile(rows, 512)
    return pl.pallas_call(
        _cast_kernel,
        out_shape=jax.ShapeDtypeStruct((rows, cols), BF16),
        grid=(rows // tr,),
        in_specs=[pl.BlockSpec((1, tr, cols), lambda i: (index, i, 0))],
        out_specs=pl.BlockSpec((tr, cols), lambda i: (i, 0)),
        compiler_params=_params("parallel"),
        name="weight_to_bf16",
    )(w)


def _norm_mm_kernel(x_ref, g_ref, sc_ref, sh_ref, w_ref, b_ref, o_ref, h_ref, mod_ref, *, gelu):
    @pl.when(pl.program_id(1) == 0)
    def _():
        _set_modulation(mod_ref, g_ref, sc_ref, sh_ref)

        def normalise(rows):
            h_ref[rows, :] = _modulated_rms_norm(x_ref[rows, :], mod_ref).astype(BF16)

        _for_each_row_group(x_ref, normalise)

    acc = jnp.dot(h_ref[...], w_ref[...], preferred_element_type=F32) + b_ref[...]
    if gelu:
        acc = 0.5 * acc * (1.0 + lax.erf(acc * F32(0.7071067811865476)))
    o_ref[...] = acc.astype(o_ref.dtype)


def _norm_matmul(x, g, sc, sh, w, b, *, seq_len, gelu):
    t, d = x.shape
    n = w.shape[1]
    tm = _tile(seq_len, 1024)
    tn = _tile(n, 512)
    per_seq = seq_len // tm
    return pl.pallas_call(
        functools.partial(_norm_mm_kernel, gelu=gelu),
        out_shape=jax.ShapeDtypeStruct((t, n), BF16),
        grid=(t // tm, n // tn),
        in_specs=[
            pl.BlockSpec((tm, d), lambda i, j: (i, 0)),
            pl.BlockSpec((1, d), lambda i, j: (0, 0)),
            pl.BlockSpec((1, 1, d), lambda i, j: (i // per_seq, 0, 0)),
            pl.BlockSpec((1, 1, d), lambda i, j: (i // per_seq, 0, 0)),
            pl.BlockSpec((d, tn), lambda i, j: (0, j)),
            pl.BlockSpec((1, tn), lambda i, j: (0, j)),
        ],
        out_specs=pl.BlockSpec((tm, tn), lambda i, j: (i, j)),
        scratch_shapes=[pltpu.VMEM((tm, d), BF16), pltpu.VMEM((2, SUBLANES, d), F32)],
        compiler_params=_params("parallel", "arbitrary"),
        name="norm_matmul_gelu" if gelu else "norm_matmul",
    )(x, g, sc, sh, w, b)


def _mm_residual_kernel(a_ref, w_ref, x_ref, gate_ref, o_ref):
    y = jnp.dot(a_ref[...], w_ref[...], preferred_element_type=F32)
    o_ref[...] = x_ref[...] + gate_ref[0] * y


def _matmul_residual(a, w, x, gate, *, seq_len):
    t, k = a.shape
    d = w.shape[1]
    tm = _tile(seq_len, 1024)
    tn = _tile(d, 512)
    per_seq = seq_len // tm
    return pl.pallas_call(
        _mm_residual_kernel,
        out_shape=jax.ShapeDtypeStruct((t, d), F32),
        grid=(t // tm, d // tn),
        in_specs=[
            pl.BlockSpec((tm, k), lambda i, j: (i, 0)),
            pl.BlockSpec((k, tn), lambda i, j: (0, j)),
            pl.BlockSpec((tm, tn), lambda i, j: (i, j)),
            pl.BlockSpec((1, 1, tn), lambda i, j: (i // per_seq, 0, j)),
        ],
        out_specs=pl.BlockSpec((tm, tn), lambda i, j: (i, j)),
        compiler_params=_params("parallel", "parallel"),
        name="matmul_residual",
    )(a, w, x, gate)


def _pool_kernel(u_ref, halo_ref, wg_ref, scale_ref, o_ref, ext_ref, *, seq_len):
    tm = u_ref.shape[0]
    gw = wg_ref.shape[1]
    start = lax.rem(pl.program_id(0) * tm, seq_len)
    ext_ref[0:POOL_HALO] = jnp.where(start == 0, 0.0, halo_ref[...].astype(F32))
    ext_ref[POOL_HALO:] = u_ref[...].astype(F32)
    pos = start + lax.broadcasted_iota(I32, (tm, 1), 0)
    for g, window in enumerate(POOL_WINDOWS):
        cols = slice(g * gw, (g + 1) * gw)
        cur = ext_ref[POOL_HALO:POOL_HALO + tm, cols]
        total = cur
        for lag in range(1, window):
            total = total + ext_ref[POOL_HALO - lag:POOL_HALO - lag + tm, cols]
        inv_cnt = 1.0 / jnp.minimum(pos + 1, window).astype(F32)
        pooled = total * inv_cnt - cur
        mixed = jnp.dot(pooled.astype(BF16), wg_ref[g], preferred_element_type=F32)
        o_ref[:, cols] = (mixed * scale_ref[:, cols]).astype(o_ref.dtype)


def _pool_middle(u, w_group, scale, *, seq_len):
    t, d = u.shape
    assert w_group.shape[0] == len(POOL_WINDOWS) and max(POOL_WINDOWS) <= POOL_HALO
    tm = _tile(seq_len, 512)
    halo_blocks = tm // POOL_HALO
    return pl.pallas_call(
        functools.partial(_pool_kernel, seq_len=seq_len),
        out_shape=jax.ShapeDtypeStruct((t, d), BF16),
        grid=(t // tm,),
        in_specs=[
            pl.BlockSpec((tm, d), lambda i: (i, 0)),
            pl.BlockSpec((POOL_HALO, d), lambda i: (jnp.maximum(i * halo_blocks - 1, 0), 0)),
            pl.BlockSpec(w_group.shape, lambda i: (0, 0, 0)),
            pl.BlockSpec((1, d), lambda i: (0, 0)),
        ],
        out_specs=pl.BlockSpec((tm, d), lambda i: (i, 0)),
        scratch_shapes=[pltpu.VMEM((tm + POOL_HALO, d), F32)],
        compiler_params=_params("parallel"),
        name="pool_middle",
    )(u, u, w_group, scale)


def _sgu_kernel(u_ref, v_ref, lng_ref, lnb_ref, ws_ref, bst_ref, o_ref, vn_ref, ln_ref):
    tm, width = u_ref.shape
    n_heads, chunk, _ = ws_ref.shape
    hw = width // n_heads
    ln_ref[0] = _sublane_rows(lng_ref[...])
    ln_ref[1] = _sublane_rows(lnb_ref[...])

    def layer_norm(rows):
        v = v_ref[rows, :].astype(F32)
        vc = v - jnp.mean(v, axis=-1, keepdims=True)
        vn = vc * lax.rsqrt(jnp.mean(vc * vc, axis=-1, keepdims=True) + NORM_EPS)
        vn_ref[rows, :] = _affine_rows(vn, ln_ref[0], ln_ref[1]).astype(BF16)

    _for_each_row_group(v_ref, layer_norm)
    causal = (lax.broadcasted_iota(I32, (chunk, chunk), 0) >= lax.broadcasted_iota(I32, (chunk, chunk), 1))
    for h in range(n_heads):
        w_causal = jnp.where(causal, ws_ref[h], 0.0).astype(BF16)
        bias = bst_ref[:, h:h + 1]
        cols = slice(h * hw, (h + 1) * hw)
        for c in range(tm // chunk):
            rows = slice(c * chunk, (c + 1) * chunk)
            mixed = jnp.dot(w_causal, vn_ref[rows, cols], preferred_element_type=F32) + bias
            o_ref[rows, cols] = (u_ref[rows, cols].astype(F32) * mixed).astype(o_ref.dtype)


def _sgu_middle(uv, ln_g, ln_b, w_s, b_s_t, *, seq_len):
    t, two_w = uv.shape
    width = two_w // 2
    chunk = w_s.shape[1]
    tm = _tile(seq_len, 4 * chunk)
    return pl.pallas_call(
        _sgu_kernel,
        out_shape=jax.ShapeDtypeStruct((t, width), BF16),
        grid=(t // tm,),
        in_specs=[
            pl.BlockSpec((tm, width), lambda i: (i, 0)),
            pl.BlockSpec((tm, width), lambda i: (i, 1)),
            pl.BlockSpec((1, width), lambda i: (0, 0)),
            pl.BlockSpec((1, width), lambda i: (0, 0)),
            pl.BlockSpec(w_s.shape, lambda i: (0, 0, 0)),
            pl.BlockSpec(b_s_t.shape, lambda i: (0, 0)),
        ],
        out_specs=pl.BlockSpec((tm, width), lambda i: (i, 0)),
        scratch_shapes=[pltpu.VMEM((tm, width), BF16), pltpu.VMEM((2, SUBLANES, width), F32)],
        compiler_params=_params("parallel"),
        name="sgu_middle",
    )(uv, uv, ln_g, ln_b, w_s, b_s_t)


def _router_kernel(x_ref, g_ref, sc_ref, sh_ref, wr_ref, br_ref,
                   hp_ref, gates_ref, eid_ref, rank_ref, counts_ref,
                   carry_ref, h_hi_ref, h_lo_ref, mod_ref, *, n_experts):
    tm = x_ref.shape[0]

    @pl.when(pl.program_id(0) == 0)
    def _():
        carry_ref[...] = jnp.zeros_like(carry_ref)

    _set_modulation(mod_ref, g_ref, sc_ref, sh_ref)

    def normalise(rows):
        h = _modulated_rms_norm(x_ref[rows, :], mod_ref)
        hp_ref[rows, :] = _pack_halves(h)
        h_hi = h.astype(BF16)
        h_hi_ref[rows, :] = h_hi
        h_lo_ref[rows, :] = (h - h_hi.astype(F32)).astype(BF16)

    _for_each_row_group(x_ref, normalise)

    lane = lax.broadcasted_iota(I32, (tm, LANES), 1)
    w_lane = lax.broadcasted_iota(I32, wr_ref.shape, 1)
    w = wr_ref[...]
    w_hi = w.astype(BF16)
    w_lo = (w - w_hi.astype(F32)).astype(BF16)
    first = jnp.dot(h_hi_ref[...], jnp.where(w_lane < n_experts, w_hi, w_lo), preferred_element_type=F32)
    second = jnp.dot(h_lo_ref[...], w_hi, preferred_element_type=F32)
    logits = first + second + pltpu.roll(first, LANES - n_experts, 1) + br_ref[...]
    remaining = jnp.where(lane < n_experts, logits, -jnp.inf)
    selected = jnp.zeros((tm, LANES), F32)
    top_vals, top_ids = [], []
    for _ in range(TOP_K):
        best = jnp.max(remaining, axis=-1, keepdims=True)
        idx = jnp.min(jnp.where(remaining == best, lane, LANES), axis=-1, keepdims=True)
        hit = lane == idx
        selected = jnp.where(hit, 1.0, selected)
        remaining = jnp.where(hit, -jnp.inf, remaining)
        top_vals.append(best)
        top_ids.append(idx)
    exps = [jnp.exp(v - top_vals[0]) for v in top_vals]
    denom = exps[0]
    for e in exps[1:]:
        denom = denom + e

    lower = (lax.broadcasted_iota(I32, (tm, tm), 0) >= lax.broadcasted_iota(I32, (tm, tm), 1))
    incl = jnp.dot(lower.astype(BF16), selected.astype(BF16), preferred_element_type=F32)
    before = incl - selected + carry_ref[...]
    gates = jnp.zeros((tm, LANES), F32)
    eids = jnp.zeros((tm, LANES), I32)
    ranks = jnp.zeros((tm, LANES), I32)
    for k in range(TOP_K):
        rank_k = jnp.sum(jnp.where(lane == top_ids[k], before, 0.0), axis=-1, keepdims=True)
        gates = jnp.where(lane == k, exps[k] / denom, gates)
        eids = jnp.where(lane == k, top_ids[k], eids)
        ranks = jnp.where(lane == k, rank_k.astype(I32), ranks)
    gates_ref[...] = gates
    eid_ref[...] = eids
    rank_ref[...] = ranks
    carry_ref[...] = carry_ref[...] + incl[tm - 1:tm, :]
    counts_ref[...] = carry_ref[...].astype(I32)


def _route(x, g, sc, sh, w_router_pad, b_router_pad, *, seq_len, n_experts):
    t, d = x.shape
    tm = _tile(seq_len, 512)
    per_seq = seq_len // tm
    tok_spec = pl.BlockSpec((tm, LANES), lambda i: (i, 0))
    return pl.pallas_call(
        functools.partial(_router_kernel, n_experts=n_experts),
        out_shape=[
            jax.ShapeDtypeStruct((t, d // 2), U32),
            jax.ShapeDtypeStruct((t, LANES), F32),
            jax.ShapeDtypeStruct((t, LANES), I32),
            jax.ShapeDtypeStruct((t, LANES), I32),
            jax.ShapeDtypeStruct((1, LANES), I32),
        ],
        grid=(t // tm,),
        in_specs=[
            pl.BlockSpec((tm, d), lambda i: (i, 0)),
            pl.BlockSpec((1, d), lambda i: (0, 0)),
            pl.BlockSpec((1, 1, d), lambda i: (i // per_seq, 0, 0)),
            pl.BlockSpec((1, 1, d), lambda i: (i // per_seq, 0, 0)),
            pl.BlockSpec((d, LANES), lambda i: (0, 0)),
            pl.BlockSpec((1, LANES), lambda i: (0, 0)),
        ],
        out_specs=[
            pl.BlockSpec((tm, d // 2), lambda i: (i, 0)),
            tok_spec, tok_spec, tok_spec,
            pl.BlockSpec((1, LANES), lambda i: (0, 0)),
        ],
        scratch_shapes=[pltpu.VMEM((1, LANES), F32), pltpu.VMEM((tm, d), BF16), pltpu.VMEM((tm, d), BF16),
                        pltpu.VMEM((2, SUBLANES, d), F32)],
        compiler_params=_params("arbitrary"),
        name="moe_router",
    )(x, g, sc, sh, w_router_pad, b_router_pad)


def _dispatch_kernel(dest_ref, pad_lo_ref, pad_hi_ref, hp_ref, xs_ref, zero_ref, sems, pad_sem, *, tm):
    base = pl.program_id(0) * tm

    @pl.when(pl.program_id(0) == 0)
    def _():
        zero_ref[...] = jnp.zeros_like(zero_ref)
        chunk = zero_ref.shape[0]

        def chunk_copy(hi, i):
            start = pl.multiple_of(hi - (i + 1) * chunk, chunk)
            return pltpu.make_async_copy(zero_ref, xs_ref.at[pl.ds(start, chunk)], pad_sem)

        def row_copy(r):
            return pltpu.make_async_copy(zero_ref.at[pl.ds(0, 1)], xs_ref.at[pl.ds(r, 1)], pad_sem)

        def for_each_copy(act):
            for e in range(pad_lo_ref.shape[0]):
                lo, hi = pad_lo_ref[e], pad_hi_ref[e]
                n_chunks = (hi - lo) // chunk
                lax.fori_loop(0, n_chunks, lambda i, c: (act(chunk_copy(hi, i)), c)[1], 0)
                lax.fori_loop(lo, hi - n_chunks * chunk, lambda r, c: (act(row_copy(r)), c)[1], 0)

        for_each_copy(lambda cp: cp.start())
        for_each_copy(lambda cp: cp.wait())

    step = pl.program_id(0)
    parity = lax.rem(step, 2)

    def issue(t, carry):
        for k in range(TOP_K):
            d = dest_ref[(base + t) * TOP_K + k]
            pltpu.make_async_copy(hp_ref.at[pl.ds(base + t, 1)], xs_ref.at[pl.ds(d, 1)], sems.at[parity]).start()
        return carry

    lax.fori_loop(0, tm, issue, 0, unroll=DMA_ISSUE_UNROLL)

    def wait_step(which):
        for _ in range(TOP_K):
            pltpu.make_async_copy(hp_ref.at[pl.ds(0, tm)], xs_ref.at[pl.ds(0, tm)], sems.at[which]).wait()

    @pl.when(step > 0)
    def _():
        wait_step(1 - parity)

    @pl.when(step == pl.num_programs(0) - 1)
    def _():
        wait_step(parity)


def _dispatch(dest, pad_lo, pad_hi, hp, n_rows, *, tm):
    t, half = hp.shape
    return pl.pallas_call(
        functools.partial(_dispatch_kernel, tm=tm),
        out_shape=jax.ShapeDtypeStruct((n_rows, half), U32),
        grid_spec=pltpu.PrefetchScalarGridSpec(
            num_scalar_prefetch=3,
            grid=(t // tm,),
            in_specs=[pl.BlockSpec(memory_space=pl.ANY)],
            out_specs=pl.BlockSpec(memory_space=pl.ANY),
            scratch_shapes=[pltpu.VMEM((ZERO_CHUNK_ROWS, half), U32),
                            pltpu.SemaphoreType.DMA((2,)), pltpu.SemaphoreType.DMA(())],
        ),
        compiler_params=_params("arbitrary"),
        name="moe_dispatch",
    )(dest, pad_lo, pad_hi, hp)


def _expert_kernel(blk_expert_ref, next_expert_ref, n_used_ref, xs_ref, wu_hbm, bu_ref, wd_hbm, bd_ref, y_ref,
                   wu_f32, wd_f32, wu_bf, wd_bf, xb_ref, sems, *, layer):
    half = xs_ref.shape[1]
    f = wd_bf.shape[0]
    b = pl.program_id(0)
    live = b < n_used_ref[0]
    expert = blk_expert_ref[b]
    opens_expert = jnp.logical_or(b == 0, blk_expert_ref[jnp.maximum(b - 1, 0)] != expert)

    def weight_copies(e):
        return (pltpu.make_async_copy(wu_hbm.at[layer, e], wu_f32, sems.at[0]),
                pltpu.make_async_copy(wd_hbm.at[layer, e], wd_f32, sems.at[1]))

    @pl.when(jnp.logical_and(live, opens_expert))
    def _():
        @pl.when(b == 0)
        def _():
            for cp in weight_copies(expert):
                cp.start()

        for cp in weight_copies(expert):
            cp.wait()
        wu_bf[...] = wu_f32[...].astype(BF16)
        wd_bf[...] = wd_f32[...].astype(BF16)
        upcoming = next_expert_ref[b]

        @pl.when(upcoming >= 0)
        def _():
            for cp in weight_copies(upcoming):
                cp.start()

    @pl.when(live)
    def _():
        lo, hi = _unpack_halves(xs_ref[...])
        xb_ref[:, :half] = lo.astype(BF16)
        xb_ref[:, half:] = hi.astype(BF16)
        rows_all = xs_ref.shape[0]
        sub = rows_all // 2
        wide = 2 * MXU_COLS
        row_halves = [slice(s * sub, (s + 1) * sub) for s in range(2)]
        wide = min(wide, wu_bf.shape[1])
        hu_parts = [[jnp.dot(xb_ref[rows, :], wu_bf[:, :wide], preferred_element_type=F32)] for rows in row_halves]
        if wide < wu_bf.shape[1]:
            for s, rows in enumerate(row_halves):
                hu_parts[s].append(jnp.dot(xb_ref[rows, :], wu_bf[:, wide:], preferred_element_type=F32))
        for s, rows in enumerate(row_halves):
            hu = jnp.concatenate(hu_parts[s], axis=1) + bu_ref[0]
            glu = jnp.minimum(hu[:, :f], SWIGLU_LIMIT)
            lin = jnp.clip(hu[:, f:], -SWIGLU_LIMIT, SWIGLU_LIMIT)
            act = (glu * jax.nn.sigmoid(SWIGLU_ALPHA * glu) * (lin + 1.0)).astype(BF16)
            cw = _tile(half, 512)
            for c in range(half // cw):
                pair = []
                for cols in (slice(c * cw, (c + 1) * cw), slice(half + c * cw, half + (c + 1) * cw)):
                    y = jnp.dot(act, wd_bf[:, cols], preferred_element_type=F32) + bd_ref[0, :, cols]
                    pair.append(lax.bitcast_convert_type(y.astype(BF16).astype(F32), U32))
                y_ref[rows, c * cw:(c + 1) * cw] = (pair[0] >> 16) | (pair[1] & U32(HI_MASK))

    @pl.when(jnp.logical_not(live))
    def _():
        y_ref[...] = jnp.zeros_like(y_ref)


def _experts(blk_expert, next_expert, n_used, xs, w_up, b_up, w_down, b_down, *, layer, block_rows):
    n_rows, half = xs.shape
    _, n_experts, d, f2 = w_up.shape
    f = f2 // 2
    n_blocks = n_rows // block_rows

    def row_block(b, blk_expert, next_expert, n_used):
        return (jnp.minimum(b, n_used[0] - 1), 0)

    def expert_block(b, blk_expert, next_expert, n_used):
        return (blk_expert[b], 0, 0)

    return pl.pallas_call(
        functools.partial(_expert_kernel, layer=layer),
        out_shape=jax.ShapeDtypeStruct((n_rows, half), U32),
        grid_spec=pltpu.PrefetchScalarGridSpec(
            num_scalar_prefetch=3,
            grid=(n_blocks,),
            in_specs=[
                pl.BlockSpec((block_rows, half), row_block),
                pl.BlockSpec(memory_space=pl.ANY),
                pl.BlockSpec((1, 1, f2), expert_block),
                pl.BlockSpec(memory_space=pl.ANY),
                pl.BlockSpec((1, 1, d), expert_block),
            ],
            out_specs=pl.BlockSpec((block_rows, half), lambda b, *_: (b, 0)),
            scratch_shapes=[
                pltpu.VMEM((d, f2), F32), pltpu.VMEM((f, d), F32),
                pltpu.VMEM((d, f2), BF16), pltpu.VMEM((f, d), BF16),
                pltpu.VMEM((block_rows, d), BF16),
                pltpu.SemaphoreType.DMA((2,)),
            ],
        ),
        compiler_params=_params("arbitrary"),
        name="moe_experts",
    )(blk_expert, next_expert, n_used, xs, w_up, b_up.reshape(n_experts, 1, f2),
      w_down, b_down.reshape(n_experts, 1, d))


def _combine_kernel(dest_ref, yb_ref, x_ref, gates_ref, gt_ref, o_ref, buf_ref, sems):
    tm = x_ref.shape[0]
    half = buf_ref.shape[3]
    step = pl.program_id(0)
    slot = lax.rem(step, 2)

    def request_rows(tile, into, group):
        for j in range(SUBLANES):
            t = group * SUBLANES + j
            for k in range(TOP_K):
                d = dest_ref[(tile * tm + t) * TOP_K + k]
                pltpu.make_async_copy(yb_ref.at[pl.ds(d, 1)], buf_ref.at[into, k, pl.ds(t, 1)],
                                      sems.at[into]).start(priority=k % 2)

    def weighted_sum(group):
        rows = pl.ds(pl.multiple_of(group * SUBLANES, SUBLANES), SUBLANES)
        acc_lo = acc_hi = None
        for k in range(TOP_K):
            lo, hi = _unpack_halves(buf_ref[slot, k, rows, :])
            gate = gates_ref[rows, k:k + 1]
            acc_lo = gate * lo if acc_lo is None else acc_lo + gate * lo
            acc_hi = gate * hi if acc_hi is None else acc_hi + gate * hi
        o_ref[rows, :half] = x_ref[rows, :half] + gt_ref[0, :, :half] * acc_lo
        o_ref[rows, half:] = x_ref[rows, half:] + gt_ref[0, :, half:] * acc_hi

    n_groups = tm // SUBLANES

    @pl.when(step == 0)
    def _():
        lax.fori_loop(0, n_groups, lambda g, c: (request_rows(0, 0, g), c)[1], 0)

    for k in range(TOP_K):
        pltpu.make_async_copy(yb_ref.at[pl.ds(0, tm)], buf_ref.at[slot, k], sems.at[slot]).wait()

    has_next = step + 1 < pl.num_programs(0)

    @pl.when(has_next)
    def _():
        def body(g, c):
            request_rows(step + 1, 1 - slot, g)
            weighted_sum(g)
            return c

        lax.fori_loop(0, n_groups, body, 0)

    @pl.when(jnp.logical_not(has_next))
    def _():
        lax.fori_loop(0, n_groups, lambda g, c: (weighted_sum(g), c)[1], 0)


def _combine(dest, yb, x, gates, gt, *, seq_len):
    t, d = x.shape
    half = d // 2
    tm = _tile(seq_len, 256)
    per_seq = seq_len // tm
    return pl.pallas_call(
        _combine_kernel,
        out_shape=jax.ShapeDtypeStruct((t, d), F32),
        grid_spec=pltpu.PrefetchScalarGridSpec(
            num_scalar_prefetch=1,
            grid=(t // tm,),
            in_specs=[
                pl.BlockSpec(memory_space=pl.ANY),
                pl.BlockSpec((tm, d), lambda i, dest: (i, 0)),
                pl.BlockSpec((tm, LANES), lambda i, dest: (i, 0)),
                pl.BlockSpec((1, 1, d), lambda i, dest: (i // per_seq, 0, 0)),
            ],
            out_specs=pl.BlockSpec((tm, d), lambda i, dest: (i, 0)),
            scratch_shapes=[pltpu.VMEM((2, TOP_K, tm, half), U32), pltpu.SemaphoreType.DMA((2,))],
        ),
        input_output_aliases={2: 0},
        compiler_params=_params("arbitrary"),
        name="moe_combine",
    )(dest, yb, x, gates, gt)


def _moe_layer(x, g, sc, sh, gt, w_router, b_router, w_up, b_up, w_down, b_down, *, layer, seq_len, block_rows):
    t, d = x.shape
    n_experts = w_router.shape[1]
    assert 2 * n_experts <= LANES
    w_router_pad = jnp.pad(jnp.concatenate([w_router, w_router], axis=1), ((0, 0), (0, LANES - 2 * n_experts)))
    b_router_pad = jnp.pad(b_router, (0, LANES - n_experts)).reshape(1, LANES)
    hp, gates, eids, ranks, counts = _route(x, g, sc, sh, w_router_pad, b_router_pad,
                                            seq_len=seq_len, n_experts=n_experts)
    counts = counts[0, :n_experts]
    padded = (counts + block_rows - 1) // block_rows * block_rows
    seg_end = jnp.cumsum(padded)
    seg_start = seg_end - padded
    n_rows = t * TOP_K + n_experts * block_rows
    n_blocks = n_rows // block_rows
    expert_ids = jnp.arange(n_experts, dtype=I32)
    is_expert = eids[:, :TOP_K, None] == expert_ids
    dest = (ranks[:, :TOP_K] + jnp.sum(jnp.where(is_expert, seg_start, 0), axis=-1)).reshape(-1).astype(I32)
    blk_start = jnp.arange(n_blocks, dtype=I32) * block_rows
    blk_expert = jnp.minimum(jnp.sum(blk_start[:, None] >= seg_end[None, :], axis=1), n_experts - 1).astype(I32)
    n_used = (seg_end[-1:] // block_rows).astype(I32)
    later = lax.cummin(jnp.where(counts > 0, expert_ids, n_experts), axis=0, reverse=True)
    following = jnp.concatenate([later[1:], jnp.full((1,), n_experts, I32)])
    next_expert = jnp.where(following < n_experts, following, -1)[blk_expert].astype(I32)

    assert block_rows % ZERO_CHUNK_ROWS == 0
    pad_lo = jnp.concatenate([seg_start + counts, seg_end[-1:]]).astype(I32)
    pad_hi = jnp.concatenate([seg_end, jnp.full((1,), n_rows, I32)]).astype(I32)
    xs = _dispatch(dest, pad_lo, pad_hi, hp, n_rows, tm=_tile(t, 256))
    yb = _experts(blk_expert, next_expert, n_used, xs, w_up, b_up, w_down, b_down,
                  layer=layer, block_rows=block_rows)
    return _combine(dest, yb, x, gates, gt, seq_len=seq_len)


def _final_norm_kernel(x_ref, g_ref, o_ref, gain_ref):
    gain_ref[...] = _sublane_rows(g_ref[...])

    def normalise(rows):
        o_ref[rows, :] = _affine_rows(_rms_normalise(x_ref[rows, :]), gain_ref[...], None)

    _for_each_row_group(x_ref, normalise)


def _final_norm(x, g):
    t, d = x.shape
    tm = _tile(t, 512)
    return pl.pallas_call(
        _final_norm_kernel,
        out_shape=jax.ShapeDtypeStruct((t, d), F32),
        grid=(t // tm,),
        in_specs=[pl.BlockSpec((tm, d), lambda i: (i, 0)), pl.BlockSpec((1, d), lambda i: (0, 0))],
        out_specs=pl.BlockSpec((tm, d), lambda i: (i, 0)),
        scratch_shapes=[pltpu.VMEM((SUBLANES, d), F32)],
        compiler_params=_params("parallel"),
        name="final_norm",
    )(x, g)


def kernel(x, c, w_cond, b_cond, w_mod, b_mod, g_norm_mix, g_norm_ffn, pool_w_in, pool_w_group, pool_scale, pool_w_out, gmlp_w_in, gmlp_b_in, gmlp_ln_g, gmlp_ln_b, gmlp_w_s, gmlp_b_s, gmlp_w_out, moe_w_router, moe_b_router, moe_w_up, moe_b_up, moe_w_down, moe_b_down, g_final):
    bsz, seq_len, d = x.shape
    depth = w_mod.shape[0]
    mod = _modulation(c, w_cond, b_cond, w_mod, b_mod)
    xt = x.reshape(bsz * seq_len, d)
    zero_bias = jnp.zeros((1, d), F32)
    for layer in range(depth):
        sh1, sc1, gt1, sh2, sc2, gt2 = (mod[layer, :, m] for m in range(6))
        j = layer // 2
        g_mix = g_norm_mix[layer].reshape(1, d)
        if layer % 2 == 0:
            u = _norm_matmul(xt, g_mix, sc1, sh1, _bf16_weight(pool_w_in, j), zero_bias,
                             seq_len=seq_len, gelu=False)
            n_groups, gw, _ = pool_w_group.shape[1:]
            w_group = _bf16_weight(pool_w_group.reshape(-1, n_groups * gw, gw), j).reshape(n_groups, gw, gw)
            mid = _pool_middle(u, w_group, pool_scale[j].reshape(1, d), seq_len=seq_len)
            w_out = _bf16_weight(pool_w_out, j)
        else:
            uv = _norm_matmul(xt, g_mix, sc1, sh1, _bf16_weight(gmlp_w_in, j), gmlp_b_in[j].reshape(1, -1),
                              seq_len=seq_len, gelu=True)
            mid = _sgu_middle(uv, gmlp_ln_g[j].reshape(1, -1), gmlp_ln_b[j].reshape(1, -1),
                              gmlp_w_s[j], gmlp_b_s[j].T, seq_len=seq_len)
            w_out = _bf16_weight(gmlp_w_out, j)
        xt = _matmul_residual(mid, w_out, xt, gt1, seq_len=seq_len)
        xt = _moe_layer(xt, g_norm_ffn[layer].reshape(1, d), sc2, sh2, gt2,
                        moe_w_router[layer], moe_b_router[layer],
                        moe_w_up, moe_b_up[layer], moe_w_down, moe_b_down[layer],
                        layer=layer, seq_len=seq_len, block_rows=512)
    return _final_norm(xt, g_final.reshape(1, d)).reshape(bsz, seq_len, d)
```

```python
import functools

import jax
import jax.numpy as jnp
from jax import lax
from jax.experimental import pallas as pl
from jax.experimental.pallas import tpu as pltpu

F32 = jnp.float32
BF16 = jnp.bfloat16
U32 = jnp.uint32
I32 = jnp.int32
HIGHEST = lax.Precision.HIGHEST

NORM_EPS = 1e-5
POOL_WINDOWS = (2, 4, 8, 16)
TOP_K = 4
SWIGLU_LIMIT = 7.0
SWIGLU_ALPHA = 1.702

LANES = 128
SUBLANES = 8
MXU_COLS = 256
POOL_HALO = 16
ZERO_CHUNK_ROWS = 64
DISPATCH_SLOTS = 3
DMA_ISSUE_UNROLL = 8
VMEM_LIMIT_BYTES = 56 * 1024 * 1024
HI_MASK = 0xFFFF0000


def _tile(n, preferred):
    if n <= preferred:
        return n
    t = preferred - preferred % LANES
    while n % t:
        t -= LANES
    assert t > 0, (n, preferred)
    return t


def _params(*semantics):
    return pltpu.CompilerParams(dimension_semantics=semantics, vmem_limit_bytes=VMEM_LIMIT_BYTES)


def _modulated_rms_norm(x, g, sc, sh):
    y = x * lax.rsqrt(jnp.mean(x * x, axis=-1, keepdims=True) + NORM_EPS)
    return (y * g) * (1.0 + sc) + sh


def _pack_halves(v):
    half = v.shape[1] // 2
    bits = lax.bitcast_convert_type(v.astype(BF16).astype(F32), U32)
    return (bits[:, :half] >> 16) | (bits[:, half:] & U32(HI_MASK))


def _unpack_halves(w):
    lo = lax.bitcast_convert_type(w << 16, F32)
    hi = lax.bitcast_convert_type(w & U32(HI_MASK), F32)
    return lo, hi


def _cond_kernel(c_ref, w_ref, b_ref, o_ref):
    z = jnp.dot(c_ref[...], w_ref[...], precision=HIGHEST, preferred_element_type=F32) + b_ref[...]
    o_ref[...] = z * jax.nn.sigmoid(z)


def _mod_kernel(cond_ref, w_ref, b_ref, o_ref):
    o_ref[0] = jnp.dot(cond_ref[...], w_ref[0], precision=HIGHEST, preferred_element_type=F32) + b_ref[0]


def _modulation(c, w_cond, b_cond, w_mod, b_mod):
    bsz, d = c.shape
    rank = w_cond.shape[1]
    depth, _, n6 = w_mod.shape
    rows = -(-bsz // SUBLANES) * SUBLANES
    c_pad = jnp.pad(c, ((0, rows - bsz), (0, 0)))
    cond = pl.pallas_call(
        _cond_kernel,
        out_shape=jax.ShapeDtypeStruct((rows, rank), F32),
        compiler_params=pltpu.CompilerParams(vmem_limit_bytes=VMEM_LIMIT_BYTES),
        name="cond",
    )(c_pad, w_cond, b_cond.reshape(1, rank))
    tn = _tile(n6, 2048)
    mod = pl.pallas_call(
        _mod_kernel,
        out_shape=jax.ShapeDtypeStruct((depth, rows, n6), F32),
        grid=(depth, n6 // tn),
        in_specs=[
            pl.BlockSpec((rows, rank), lambda l, j: (0, 0)),
            pl.BlockSpec((1, rank, tn), lambda l, j: (l, 0, j)),
            pl.BlockSpec((1, 1, tn), lambda l, j: (l, 0, j)),
        ],
        out_specs=pl.BlockSpec((1, rows, tn), lambda l, j: (l, 0, j)),
        compiler_params=_params("parallel", "parallel"),
        name="mod",
    )(cond, w_mod, b_mod.reshape(depth, 1, n6))
    return mod[:, :bsz].reshape(depth, bsz, n6 // d, 1, d)


def _cast_kernel(w_ref, o_ref):
    o_ref[...] = w_ref[0].astype(o_ref.dtype)


def _bf16_weight(w, index):
    _, rows, cols = w.shape
    tr = _tile(rows, 512)
    return pl.pallas_call(
        _cast_kernel,
        out_shape=jax.ShapeDtypeStruct((rows, cols), BF16),
        grid=(rows // tr,),
        in_specs=[pl.BlockSpec((1, tr, cols), lambda i: (index, i, 0))],
        out_specs=pl.BlockSpec((tr, cols), lambda i: (i, 0)),
        compiler_params=_params("parallel"),
        name="weight_to_bf16",
    )(w)


def _norm_mm_kernel(x_ref, g_ref, sc_ref, sh_ref, w_ref, b_ref, o_ref, h_ref, *, gelu):
    @pl.when(pl.program_id(1) == 0)
    def _():
        h_ref[...] = _modulated_rms_norm(x_ref[...], g_ref[...], sc_ref[0], sh_ref[0]).astype(BF16)

    acc = jnp.dot(h_ref[...], w_ref[...], preferred_element_type=F32) + b_ref[...]
    if gelu:
        acc = 0.5 * acc * (1.0 + lax.erf(acc * F32(0.7071067811865476)))
    o_ref[...] = acc.astype(o_ref.dtype)


def _norm_matmul(x, g, sc, sh, w, b, *, seq_len, gelu):
    t, d = x.shape
    n = w.shape[1]
    tm = _tile(seq_len, 512)
    tn = _tile(n, 1024)
    per_seq = seq_len // tm
    return pl.pallas_call(
        functools.partial(_norm_mm_kernel, gelu=gelu),
        out_shape=jax.ShapeDtypeStruct((t, n), BF16),
        grid=(t // tm, n // tn),
        in_specs=[
            pl.BlockSpec((tm, d), lambda i, j: (i, 0)),
            pl.BlockSpec((1, d), lambda i, j: (0, 0)),
            pl.BlockSpec((1, 1, d), lambda i, j: (i // per_seq, 0, 0)),
            pl.BlockSpec((1, 1, d), lambda i, j: (i // per_seq, 0, 0)),
            pl.BlockSpec((d, tn), lambda i, j: (0, j)),
            pl.BlockSpec((1, tn), lambda i, j: (0, j)),
        ],
        out_specs=pl.BlockSpec((tm, tn), lambda i, j: (i, j)),
        scratch_shapes=[pltpu.VMEM((tm, d), BF16)],
        compiler_params=_params("parallel", "arbitrary"),
        name="norm_matmul_gelu" if gelu else "norm_matmul",
    )(x, g, sc, sh, w, b)


def _mm_residual_kernel(a_ref, w_ref, x_ref, gate_ref, o_ref):
    y = jnp.dot(a_ref[...], w_ref[...], preferred_element_type=F32)
    o_ref[...] = x_ref[...] + gate_ref[0] * y


def _matmul_residual(a, w, x, gate, *, seq_len):
    t, k = a.shape
    d = w.shape[1]
    tm = _tile(seq_len, 1024)
    tn = _tile(d, 512)
    per_seq = seq_len // tm
    return pl.pallas_call(
        _mm_residual_kernel,
        out_shape=jax.ShapeDtypeStruct((t, d), F32),
        grid=(t // tm, d // tn),
        in_specs=[
            pl.BlockSpec((tm, k), lambda i, j: (i, 0)),
            pl.BlockSpec((k, tn), lambda i, j: (0, j)),
            pl.BlockSpec((tm, tn), lambda i, j: (i, j)),
            pl.BlockSpec((1, 1, tn), lambda i, j: (i // per_seq, 0, j)),
        ],
        out_specs=pl.BlockSpec((tm, tn), lambda i, j: (i, j)),
        compiler_params=_params("parallel", "parallel"),
        name="matmul_residual",
    )(a, w, x, gate)


def _pool_kernel(u_ref, halo_ref, wg_ref, scale_ref, o_ref, ext_ref, ping_ref, pong_ref, *, seq_len):
    tm = u_ref.shape[0]
    gw = wg_ref.shape[1]
    start = lax.rem(pl.program_id(0) * tm, seq_len)
    ext_ref[0:POOL_HALO] = jnp.where(start == 0, 0.0, halo_ref[...].astype(F32))
    ext_ref[POOL_HALO:] = u_ref[...].astype(F32)
    pos = start + lax.broadcasted_iota(I32, (tm, 1), 0)
    for g, window in enumerate(POOL_WINDOWS):
        cols = slice(g * gw, (g + 1) * gw)
        cur = ext_ref[POOL_HALO:POOL_HALO + tm, cols]
        read = lambda first, n: ext_ref[POOL_HALO + first:POOL_HALO + first + n, cols]
        spare = [ping_ref, pong_ref]
        w = 1
        while True:
            history = window - 2 * w
            total = read(-history, tm + history) + read(-history - w, tm + history)
            w *= 2
            if w == window:
                break
            held = spare[0]
            spare.reverse()
            held[0:tm + history, :] = total
            read = lambda first, n, held=held, history=history: held[history + first:history + first + n, :]
        inv_cnt = 1.0 / jnp.minimum(pos + 1, window).astype(F32)
        pooled = total * inv_cnt - cur
        mixed = jnp.dot(pooled.astype(BF16), wg_ref[g], preferred_element_type=F32)
        o_ref[:, cols] = (mixed * scale_ref[:, cols]).astype(o_ref.dtype)


def _pool_middle(u, w_group, scale, *, seq_len):
    t, d = u.shape
    assert w_group.shape[0] == len(POOL_WINDOWS) and max(POOL_WINDOWS) <= POOL_HALO
    tm = _tile(seq_len, 512)
    halo_blocks = tm // POOL_HALO
    return pl.pallas_call(
        functools.partial(_pool_kernel, seq_len=seq_len),
        out_shape=jax.ShapeDtypeStruct((t, d), BF16),
        grid=(t // tm,),
        in_specs=[
            pl.BlockSpec((tm, d), lambda i: (i, 0)),
            pl.BlockSpec((POOL_HALO, d), lambda i: (jnp.maximum(i * halo_blocks - 1, 0), 0)),
            pl.BlockSpec(w_group.shape, lambda i: (0, 0, 0)),
            pl.BlockSpec((1, d), lambda i: (0, 0)),
        ],
        out_specs=pl.BlockSpec((tm, d), lambda i: (i, 0)),
        scratch_shapes=[pltpu.VMEM((tm + POOL_HALO, d), F32),
                        pltpu.VMEM((tm + POOL_HALO, d // len(POOL_WINDOWS)), F32),
                        pltpu.VMEM((tm + POOL_HALO, d // len(POOL_WINDOWS)), F32)],
        compiler_params=_params("parallel"),
        name="pool_middle",
    )(u, u, w_group, scale)


def _sgu_kernel(u_ref, v_ref, lng_ref, lnb_ref, ws_ref, bst_ref, o_ref, vn_ref):
    tm, width = u_ref.shape
    n_heads, chunk, _ = ws_ref.shape
    hw = width // n_heads
    v = v_ref[...].astype(F32)
    vc = v - jnp.mean(v, axis=-1, keepdims=True)
    vn = vc * lax.rsqrt(jnp.mean(vc * vc, axis=-1, keepdims=True) + NORM_EPS)
    vn_ref[...] = (vn * lng_ref[...] + lnb_ref[...]).astype(BF16)
    causal = (lax.broadcasted_iota(I32, (chunk, chunk), 0) >= lax.broadcasted_iota(I32, (chunk, chunk), 1))
    for h in range(n_heads):
        w_causal = jnp.where(causal, ws_ref[h], 0.0).astype(BF16)
        bias = bst_ref[:, h:h + 1]
        cols = slice(h * hw, (h + 1) * hw)
        for c in range(tm // chunk):
            rows = slice(c * chunk, (c + 1) * chunk)
            mixed = jnp.dot(w_causal, vn_ref[rows, cols], preferred_element_type=F32) + bias
            o_ref[rows, cols] = (u_ref[rows, cols].astype(F32) * mixed).astype(o_ref.dtype)


def _sgu_middle(uv, ln_g, ln_b, w_s, b_s_t, *, seq_len):
    t, two_w = uv.shape
    width = two_w // 2
    chunk = w_s.shape[1]
    tm = _tile(seq_len, 4 * chunk)
    return pl.pallas_call(
        _sgu_kernel,
        out_shape=jax.ShapeDtypeStruct((t, width), BF16),
        grid=(t // tm,),
        in_specs=[
            pl.BlockSpec((tm, width), lambda i: (i, 0)),
            pl.BlockSpec((tm, width), lambda i: (i, 1)),
            pl.BlockSpec((1, width), lambda i: (0, 0)),
            pl.BlockSpec((1, width), lambda i: (0, 0)),
            pl.BlockSpec(w_s.shape, lambda i: (0, 0, 0)),
            pl.BlockSpec(b_s_t.shape, lambda i: (0, 0)),
        ],
        out_specs=pl.BlockSpec((tm, width), lambda i: (i, 0)),
        scratch_shapes=[pltpu.VMEM((tm, width), BF16)],
        compiler_params=_params("parallel"),
        name="sgu_middle",
    )(uv, uv, ln_g, ln_b, w_s, b_s_t)


def _router_kernel(x_ref, g_ref, sc_ref, sh_ref, wr_ref, br_ref,
                   hp_ref, gates_ref, eid_ref, rank_ref, counts_ref, carry_ref, *, n_experts):
    tm = x_ref.shape[0]

    @pl.when(pl.program_id(0) == 0)
    def _():
        carry_ref[...] = jnp.zeros_like(carry_ref)

    h = _modulated_rms_norm(x_ref[...], g_ref[...], sc_ref[0], sh_ref[0])
    hp_ref[...] = _pack_halves(h)

    lane = lax.broadcasted_iota(I32, (tm, LANES), 1)
    w_lane = lax.broadcasted_iota(I32, wr_ref.shape, 1)
    w = wr_ref[...]
    w_hi = w.astype(BF16)
    w_lo = (w - w_hi.astype(F32)).astype(BF16)
    h_hi = h.astype(BF16)
    h_lo = (h - h_hi.astype(F32)).astype(BF16)
    first = jnp.dot(h_hi, jnp.where(w_lane < n_experts, w_hi, w_lo), preferred_element_type=F32)
    second = jnp.dot(h_lo, w_hi, preferred_element_type=F32)
    logits = first + second + pltpu.roll(first, LANES - n_experts, 1) + br_ref[...]
    remaining = jnp.where(lane < n_experts, logits, -jnp.inf)
    selected = jnp.zeros((tm, LANES), F32)
    top_vals, top_ids = [], []
    for _ in range(TOP_K):
        best = jnp.max(remaining, axis=-1, keepdims=True)
        idx = jnp.min(jnp.where(remaining == best, lane, LANES), axis=-1, keepdims=True)
        hit = lane == idx
        selected = jnp.where(hit, 1.0, selected)
        remaining = jnp.where(hit, -jnp.inf, remaining)
        top_vals.append(best)
        top_ids.append(idx)
    exps = [jnp.exp(v - top_vals[0]) for v in top_vals]
    denom = exps[0]
    for e in exps[1:]:
        denom = denom + e

    lower = (lax.broadcasted_iota(I32, (tm, tm), 0) >= lax.broadcasted_iota(I32, (tm, tm), 1))
    incl = jnp.dot(lower.astype(BF16), selected.astype(BF16), preferred_element_type=F32)
    before = incl - selected + carry_ref[...]
    gates = jnp.zeros((tm, LANES), F32)
    eids = jnp.zeros((tm, LANES), I32)
    ranks = jnp.zeros((tm, LANES), I32)
    for k in range(TOP_K):
        rank_k = jnp.sum(jnp.where(lane == top_ids[k], before, 0.0), axis=-1, keepdims=True)
        gates = jnp.where(lane == k, exps[k] / denom, gates)
        eids = jnp.where(lane == k, top_ids[k], eids)
        ranks = jnp.where(lane == k, rank_k.astype(I32), ranks)
    gates_ref[...] = gates
    eid_ref[...] = eids
    rank_ref[...] = ranks
    carry_ref[...] = carry_ref[...] + incl[tm - 1:tm, :]
    counts_ref[...] = carry_ref[...].astype(I32)


def _route(x, g, sc, sh, w_router_pad, b_router_pad, *, seq_len, n_experts):
    t, d = x.shape
    tm = _tile(seq_len, 512)
    per_seq = seq_len // tm
    tok_spec = pl.BlockSpec((tm, LANES), lambda i: (i, 0))
    return pl.pallas_call(
        functools.partial(_router_kernel, n_experts=n_experts),
        out_shape=[
            jax.ShapeDtypeStruct((t, d // 2), U32),
            jax.ShapeDtypeStruct((t, LANES), F32),
            jax.ShapeDtypeStruct((t, LANES), I32),
            jax.ShapeDtypeStruct((t, LANES), I32),
            jax.ShapeDtypeStruct((1, LANES), I32),
        ],
        grid=(t // tm,),
        in_specs=[
            pl.BlockSpec((tm, d), lambda i: (i, 0)),
            pl.BlockSpec((1, d), lambda i: (0, 0)),
            pl.BlockSpec((1, 1, d), lambda i: (i // per_seq, 0, 0)),
            pl.BlockSpec((1, 1, d), lambda i: (i // per_seq, 0, 0)),
            pl.BlockSpec((d, LANES), lambda i: (0, 0)),
            pl.BlockSpec((1, LANES), lambda i: (0, 0)),
        ],
        out_specs=[
            pl.BlockSpec((tm, d // 2), lambda i: (i, 0)),
            tok_spec, tok_spec, tok_spec,
            pl.BlockSpec((1, LANES), lambda i: (0, 0)),
        ],
        scratch_shapes=[pltpu.VMEM((1, LANES), F32)],
        compiler_params=_params("arbitrary"),
        name="moe_router",
    )(x, g, sc, sh, w_router_pad, b_router_pad)


def _dispatch_kernel(dest_ref, pad_lo_ref, pad_hi_ref, hp_ref, xs_ref,
                     stage_ref, zero_ref, stage_sems, row_sems, pad_sem):
    n_slots, tm, _ = stage_ref.shape
    step = pl.program_id(0)
    n_steps = pl.num_programs(0)
    slot = lax.rem(step, n_slots)
    parity = lax.rem(step, 2)

    def stage_copy(tile, into):
        rows = pl.ds(pl.multiple_of(tile * tm, tm), tm)
        return pltpu.make_async_copy(hp_ref.at[rows], stage_ref.at[into], stage_sems.at[into])

    @pl.when(step == 0)
    def _():
        stage_copy(0, 0).start()
        zero_ref[...] = jnp.zeros_like(zero_ref)
        chunk = zero_ref.shape[0]

        def chunk_copy(hi, i):
            start = pl.multiple_of(hi - (i + 1) * chunk, chunk)
            return pltpu.make_async_copy(zero_ref, xs_ref.at[pl.ds(start, chunk)], pad_sem)

        def row_copy(r):
            return pltpu.make_async_copy(zero_ref.at[pl.ds(0, 1)], xs_ref.at[pl.ds(r, 1)], pad_sem)

        def for_each_copy(act):
            for e in range(pad_lo_ref.shape[0]):
                lo, hi = pad_lo_ref[e], pad_hi_ref[e]
                n_chunks = (hi - lo) // chunk
                lax.fori_loop(0, n_chunks, lambda i, c: (act(chunk_copy(hi, i)), c)[1], 0)
                lax.fori_loop(lo, hi - n_chunks * chunk, lambda r, c: (act(row_copy(r)), c)[1], 0)

        for_each_copy(lambda cp: cp.start())
        for_each_copy(lambda cp: cp.wait())

    stage_copy(step, slot).wait()

    @pl.when(step + 1 < n_steps)
    def _():
        stage_copy(step + 1, lax.rem(step + 1, n_slots)).start()

    def issue(t, carry):
        for k in range(TOP_K):
            d = dest_ref[(step * tm + t) * TOP_K + k]
            pltpu.make_async_copy(stage_ref.at[slot, pl.ds(t, 1)], xs_ref.at[pl.ds(d, 1)],
                                  row_sems.at[parity]).start()
        return carry

    lax.fori_loop(0, tm, issue, 0, unroll=DMA_ISSUE_UNROLL)

    def wait_rows(which):
        for _ in range(TOP_K):
            pltpu.make_async_copy(stage_ref.at[0], xs_ref.at[pl.ds(0, tm)], row_sems.at[which]).wait()

    @pl.when(step > 0)
    def _():
        wait_rows(1 - parity)

    @pl.when(step == n_steps - 1)
    def _():
        wait_rows(parity)


def _dispatch(dest, pad_lo, pad_hi, hp, n_rows):
    t, half = hp.shape
    tm = _tile(t, 256)
    return pl.pallas_call(
        _dispatch_kernel,
        out_shape=jax.ShapeDtypeStruct((n_rows, half), U32),
        grid_spec=pltpu.PrefetchScalarGridSpec(
            num_scalar_prefetch=3,
            grid=(t // tm,),
            in_specs=[pl.BlockSpec(memory_space=pl.ANY)],
            out_specs=pl.BlockSpec(memory_space=pl.ANY),
            scratch_shapes=[pltpu.VMEM((DISPATCH_SLOTS, tm, half), U32),
                            pltpu.VMEM((ZERO_CHUNK_ROWS, half), U32),
                            pltpu.SemaphoreType.DMA((DISPATCH_SLOTS,)),
                            pltpu.SemaphoreType.DMA((2,)), pltpu.SemaphoreType.DMA(())],
        ),
        compiler_params=_params("arbitrary"),
        name="moe_dispatch",
    )(dest, pad_lo, pad_hi, hp)


def _expert_kernel(blk_expert_ref, next_expert_ref, n_used_ref, xs_ref, wu_hbm, bu_ref, wd_hbm, bd_ref, y_ref,
                   wu_f32, wd_f32, wu_bf, wd_bf, xb_ref, sems, *, layer):
    half = xs_ref.shape[1]
    f = wd_bf.shape[0]
    b = pl.program_id(0)
    live = b < n_used_ref[0]
    expert = blk_expert_ref[b]
    opens_expert = jnp.logical_or(b == 0, blk_expert_ref[jnp.maximum(b - 1, 0)] != expert)

    def weight_copies(e):
        return (pltpu.make_async_copy(wu_hbm.at[layer, e], wu_f32, sems.at[0]),
                pltpu.make_async_copy(wd_hbm.at[layer, e], wd_f32, sems.at[1]))

    @pl.when(jnp.logical_and(live, opens_expert))
    def _():
        @pl.when(b == 0)
        def _():
            for cp in weight_copies(expert):
                cp.start()

        for cp in weight_copies(expert):
            cp.wait()
        wu_bf[...] = wu_f32[...].astype(BF16)
        wd_bf[...] = wd_f32[...].astype(BF16)
        upcoming = next_expert_ref[b]

        @pl.when(upcoming >= 0)
        def _():
            for cp in weight_copies(upcoming):
                cp.start()

    @pl.when(live)
    def _():
        lo, hi = _unpack_halves(xs_ref[...])
        xb_ref[:, :half] = lo.astype(BF16)
        xb_ref[:, half:] = hi.astype(BF16)
        rows_all = xs_ref.shape[0]
        sub = rows_all // 2
        wide = 2 * MXU_COLS
        row_halves = [slice(s * sub, (s + 1) * sub) for s in range(2)]
        wide = min(wide, wu_bf.shape[1])
        hu_parts = [[jnp.dot(xb_ref[rows, :], wu_bf[:, :wide], preferred_element_type=F32)] for rows in row_halves]
        if wide < wu_bf.shape[1]:
            for s, rows in enumerate(row_halves):
                hu_parts[s].append(jnp.dot(xb_ref[rows, :], wu_bf[:, wide:], preferred_element_type=F32))
        for s, rows in enumerate(row_halves):
            hu = jnp.concatenate(hu_parts[s], axis=1) + bu_ref[0]
            glu = jnp.minimum(hu[:, :f], SWIGLU_LIMIT)
            lin = jnp.clip(hu[:, f:], -SWIGLU_LIMIT, SWIGLU_LIMIT)
            act = (glu * jax.nn.sigmoid(SWIGLU_ALPHA * glu) * (lin + 1.0)).astype(BF16)
            cw = _tile(half, 512)
            for c in range(half // cw):
                pair = []
                for cols in (slice(c * cw, (c + 1) * cw), slice(half + c * cw, half + (c + 1) * cw)):
                    y = jnp.dot(act, wd_bf[:, cols], preferred_element_type=F32) + bd_ref[0, :, cols]
                    pair.append(lax.bitcast_convert_type(y.astype(BF16).astype(F32), U32))
                y_ref[rows, c * cw:(c + 1) * cw] = (pair[0] >> 16) | (pair[1] & U32(HI_MASK))

    @pl.when(jnp.logical_not(live))
    def _():
        y_ref[...] = jnp.zeros_like(y_ref)


def _experts(blk_expert, next_expert, n_used, xs, w_up, b_up, w_down, b_down, *, layer, block_rows):
    n_rows, half = xs.shape
    _, n_experts, d, f2 = w_up.shape
    f = f2 // 2
    n_blocks = n_rows // block_rows

    def row_block(b, blk_expert, next_expert, n_used):
        return (jnp.minimum(b, n_used[0] - 1), 0)

    def expert_block(b, blk_expert, next_expert, n_used):
        return (blk_expert[b], 0, 0)

    return pl.pallas_call(
        functools.partial(_expert_kernel, layer=layer),
        out_shape=jax.ShapeDtypeStruct((n_rows, half), U32),
        grid_spec=pltpu.PrefetchScalarGridSpec(
            num_scalar_prefetch=3,
            grid=(n_blocks,),
            in_specs=[
                pl.BlockSpec((block_rows, half), row_block),
                pl.BlockSpec(memory_space=pl.ANY),
                pl.BlockSpec((1, 1, f2), expert_block),
                pl.BlockSpec(memory_space=pl.ANY),
                pl.BlockSpec((1, 1, d), expert_block),
            ],
            out_specs=pl.BlockSpec((block_rows, half), lambda b, *_: (b, 0)),
            scratch_shapes=[
                pltpu.VMEM((d, f2), F32), pltpu.VMEM((f, d), F32),
                pltpu.VMEM((d, f2), BF16), pltpu.VMEM((f, d), BF16),
                pltpu.VMEM((block_rows, d), BF16),
                pltpu.SemaphoreType.DMA((2,)),
            ],
        ),
        compiler_params=_params("arbitrary"),
        name="moe_experts",
    )(blk_expert, next_expert, n_used, xs, w_up, b_up.reshape(n_experts, 1, f2),
      w_down, b_down.reshape(n_experts, 1, d))


def _combine_kernel(dest_ref, yb_ref, x_ref, gates_ref, gt_ref, o_ref, buf_ref, sems):
    tm = x_ref.shape[0]
    half = buf_ref.shape[3]
    step = pl.program_id(0)
    slot = lax.rem(step, 2)

    n_groups = tm // SUBLANES
    half_groups = n_groups // 2
    half_rows = tm // 2

    def request_rows(tile, into, group, part):
        for j in range(SUBLANES):
            t = group * SUBLANES + j
            for k in range(TOP_K):
                d = dest_ref[(tile * tm + t) * TOP_K + k]
                pltpu.make_async_copy(yb_ref.at[pl.ds(d, 1)], buf_ref.at[into, k, pl.ds(t, 1)],
                                      sems.at[into, part]).start()

    def wait_rows(part):
        for k in range(TOP_K):
            pltpu.make_async_copy(yb_ref.at[pl.ds(0, half_rows)], buf_ref.at[slot, k, pl.ds(0, half_rows)],
                                  sems.at[slot, part]).wait()

    def weighted_sum(group):
        rows = pl.ds(pl.multiple_of(group * SUBLANES, SUBLANES), SUBLANES)
        acc_lo = acc_hi = None
        for k in range(TOP_K):
            lo, hi = _unpack_halves(buf_ref[slot, k, rows, :])
            gate = gates_ref[rows, k:k + 1]
            acc_lo = gate * lo if acc_lo is None else acc_lo + gate * lo
            acc_hi = gate * hi if acc_hi is None else acc_hi + gate * hi
        o_ref[rows, :half] = x_ref[rows, :half] + gt_ref[0, :, :half] * acc_lo
        o_ref[rows, half:] = x_ref[rows, half:] + gt_ref[0, :, half:] * acc_hi

    @pl.when(step == 0)
    def _():
        for part in range(2):
            lax.fori_loop(part * half_groups, (part + 1) * half_groups,
                          lambda g, c, part=part: (request_rows(0, 0, g, part), c)[1], 0)

    has_next = step + 1 < pl.num_programs(0)
    for part in range(2):
        wait_rows(part)
        first, last = part * half_groups, (part + 1) * half_groups

        @pl.when(has_next)
        def _():
            def body(g, c):
                request_rows(step + 1, 1 - slot, g, part)
                weighted_sum(g)
                return c

            lax.fori_loop(first, last, body, 0)

        @pl.when(jnp.logical_not(has_next))
        def _():
            lax.fori_loop(first, last, lambda g, c: (weighted_sum(g), c)[1], 0)


def _combine(dest, yb, x, gates, gt, *, seq_len):
    t, d = x.shape
    half = d // 2
    tm = _tile(seq_len, 256)
    per_seq = seq_len // tm
    return pl.pallas_call(
        _combine_kernel,
        out_shape=jax.ShapeDtypeStruct((t, d), F32),
        grid_spec=pltpu.PrefetchScalarGridSpec(
            num_scalar_prefetch=1,
            grid=(t // tm,),
            in_specs=[
                pl.BlockSpec(memory_space=pl.ANY),
                pl.BlockSpec((tm, d), lambda i, dest: (i, 0)),
                pl.BlockSpec((tm, LANES), lambda i, dest: (i, 0)),
                pl.BlockSpec((1, 1, d), lambda i, dest: (i // per_seq, 0, 0)),
            ],
            out_specs=pl.BlockSpec((tm, d), lambda i, dest: (i, 0)),
            scratch_shapes=[pltpu.VMEM((2, TOP_K, tm, half), U32), pltpu.SemaphoreType.DMA((2, 2))],
        ),
        input_output_aliases={2: 0},
        compiler_params=_params("arbitrary"),
        name="moe_combine",
    )(dest, yb, x, gates, gt)


def _moe_layer(x, g, sc, sh, gt, w_router, b_router, w_up, b_up, w_down, b_down, *, layer, seq_len, block_rows):
    t, d = x.shape
    n_experts = w_router.shape[1]
    assert 2 * n_experts <= LANES
    w_router_pad = jnp.pad(jnp.concatenate([w_router, w_router], axis=1), ((0, 0), (0, LANES - 2 * n_experts)))
    b_router_pad = jnp.pad(b_router, (0, LANES - n_experts)).reshape(1, LANES)
    hp, gates, eids, ranks, counts = _route(x, g, sc, sh, w_router_pad, b_router_pad,
                                            seq_len=seq_len, n_experts=n_experts)
    counts = counts[0, :n_experts]
    padded = (counts + block_rows - 1) // block_rows * block_rows
    seg_end = jnp.cumsum(padded)
    seg_start = seg_end - padded
    n_rows = t * TOP_K + n_experts * block_rows
    n_blocks = n_rows // block_rows
    expert_ids = jnp.arange(n_experts, dtype=I32)
    is_expert = eids[:, :TOP_K, None] == expert_ids
    dest = (ranks[:, :TOP_K] + jnp.sum(jnp.where(is_expert, seg_start, 0), axis=-1)).reshape(-1).astype(I32)
    blk_start = jnp.arange(n_blocks, dtype=I32) * block_rows
    blk_expert = jnp.minimum(jnp.sum(blk_start[:, None] >= seg_end[None, :], axis=1), n_experts - 1).astype(I32)
    n_used = (seg_end[-1:] // block_rows).astype(I32)
    later = lax.cummin(jnp.where(counts > 0, expert_ids, n_experts), axis=0, reverse=True)
    following = jnp.concatenate([later[1:], jnp.full((1,), n_experts, I32)])
    next_expert = jnp.where(following < n_experts, following, -1)[blk_expert].astype(I32)

    assert block_rows % ZERO_CHUNK_ROWS == 0
    pad_lo = jnp.concatenate([seg_start + counts, seg_end[-1:]]).astype(I32)
    pad_hi = jnp.concatenate([seg_end, jnp.full((1,), n_rows, I32)]).astype(I32)
    xs = _dispatch(dest, pad_lo, pad_hi, hp, n_rows)
    yb = _experts(blk_expert, next_expert, n_used, xs, w_up, b_up, w_down, b_down,
                  layer=layer, block_rows=block_rows)
    return _combine(dest, yb, x, gates, gt, seq_len=seq_len)


def _final_norm_kernel(x_ref, g_ref, o_ref):
    x = x_ref[...]
    o_ref[...] = x * lax.rsqrt(jnp.mean(x * x, axis=-1, keepdims=True) + NORM_EPS) * g_ref[...]


def _final_norm(x, g):
    t, d = x.shape
    tm = _tile(t, 512)
    return pl.pallas_call(
        _final_norm_kernel,
        out_shape=jax.ShapeDtypeStruct((t, d), F32),
        grid=(t // tm,),
        in_specs=[pl.BlockSpec((tm, d), lambda i: (i, 0)), pl.BlockSpec((1, d), lambda i: (0, 0))],
        out_specs=pl.BlockSpec((tm, d), lambda i: (i, 0)),
        compiler_params=_params("parallel"),
        name="final_norm",
    )(x, g)


def kernel(x, c, w_cond, b_cond, w_mod, b_mod, g_norm_mix, g_norm_ffn, pool_w_in, pool_w_group, pool_scale, pool_w_out, gmlp_w_in, gmlp_b_in, gmlp_ln_g, gmlp_ln_b, gmlp_w_s, gmlp_b_s, gmlp_w_out, moe_w_router, moe_b_router, moe_w_up, moe_b_up, moe_w_down, moe_b_down, g_final):
    bsz, seq_len, d = x.shape
    depth = w_mod.shape[0]
    mod = _modulation(c, w_cond, b_cond, w_mod, b_mod)
    xt = x.reshape(bsz * seq_len, d)
    zero_bias = jnp.zeros((1, d), F32)
    for layer in range(depth):
        sh1, sc1, gt1, sh2, sc2, gt2 = (mod[layer, :, m] for m in range(6))
        j = layer // 2
        g_mix = g_norm_mix[layer].reshape(1, d)
        if layer % 2 == 0:
            u = _norm_matmul(xt, g_mix, sc1, sh1, _bf16_weight(pool_w_in, j), zero_bias,
                             seq_len=seq_len, gelu=False)
            n_groups, gw, _ = pool_w_group.shape[1:]
            w_group = _bf16_weight(pool_w_group.reshape(-1, n_groups * gw, gw), j).reshape(n_groups, gw, gw)
            mid = _pool_middle(u, w_group, pool_scale[j].reshape(1, d), seq_len=seq_len)
            w_out = _bf16_weight(pool_w_out, j)
        else:
            uv = _norm_matmul(xt, g_mix, sc1, sh1, _bf16_weight(gmlp_w_in, j), gmlp_b_in[j].reshape(1, -1),
                              seq_len=seq_len, gelu=True)
            mid = _sgu_middle(uv, gmlp_ln_g[j].reshape(1, -1), gmlp_ln_b[j].reshape(1, -1),
                              gmlp_w_s[j], gmlp_b_s[j].T, seq_len=seq_len)
            w_out = _bf16_weight(gmlp_w_out, j)
        xt = _matmul_residual(mid, w_out, xt, gt1, seq_len=seq_len)
        xt = _moe_layer(xt, g_norm_ffn[layer].reshape(1, d), sc2, sh2, gt2,
                        moe_w_router[layer], moe_b_router[layer],
                        moe_w_up, moe_b_up[layer], moe_w_down, moe_b_down[layer],
                        layer=layer, seq_len=seq_len, block_rows=512)
    return _final_norm(xt, g_final.reshape(1, d)).reshape(bsz, seq_len, d)
```

```python
import functools

import jax
import jax.numpy as jnp
from jax import lax
from jax.experimental import pallas as pl
from jax.experimental.pallas import tpu as pltpu

F32 = jnp.float32
BF16 = jnp.bfloat16
U32 = jnp.uint32
I32 = jnp.int32
HIGHEST = lax.Precision.HIGHEST

NORM_EPS = 1e-5
POOL_WINDOWS = (2, 4, 8, 16)
TOP_K = 4
SWIGLU_LIMIT = 7.0
SWIGLU_ALPHA = 1.702

LANES = 128
SUBLANES = 8
MXU_COLS = 256
POOL_HALO = 16
ZERO_CHUNK_ROWS = 64
DISPATCH_SLOTS = 3
DMA_ISSUE_UNROLL = 8
VMEM_LIMIT_BYTES = 56 * 1024 * 1024
HI_MASK = 0xFFFF0000


def _tile(n, preferred):
    if n <= preferred:
        return n
    t = preferred - preferred % LANES
    while n % t:
        t -= LANES
    assert t > 0, (n, preferred)
    return t


def _params(*semantics):
    return pltpu.CompilerParams(dimension_semantics=semantics, vmem_limit_bytes=VMEM_LIMIT_BYTES)


def _modulated_rms_norm(x, g, sc, sh):
    y = x * lax.rsqrt(jnp.mean(x * x, axis=-1, keepdims=True) + NORM_EPS)
    return (y * g) * (1.0 + sc) + sh


def _pack_halves(v):
    half = v.shape[1] // 2
    bits = lax.bitcast_convert_type(v.astype(BF16).astype(F32), U32)
    return (bits[:, :half] >> 16) | (bits[:, half:] & U32(HI_MASK))


def _unpack_halves(w):
    lo = lax.bitcast_convert_type(w << 16, F32)
    hi = lax.bitcast_convert_type(w & U32(HI_MASK), F32)
    return lo, hi


def _cond_kernel(c_ref, w_ref, b_ref, o_ref):
    z = jnp.dot(c_ref[...], w_ref[...], precision=HIGHEST, preferred_element_type=F32) + b_ref[...]
    o_ref[...] = z * jax.nn.sigmoid(z)


def _mod_kernel(cond_ref, w_ref, b_ref, o_ref):
    o_ref[0] = jnp.dot(cond_ref[...], w_ref[0], precision=HIGHEST, preferred_element_type=F32) + b_ref[0]


def _modulation(c, w_cond, b_cond, w_mod, b_mod):
    bsz, d = c.shape
    rank = w_cond.shape[1]
    depth, _, n6 = w_mod.shape
    rows = -(-bsz // SUBLANES) * SUBLANES
    c_pad = jnp.pad(c, ((0, rows - bsz), (0, 0)))
    cond = pl.pallas_call(
        _cond_kernel,
        out_shape=jax.ShapeDtypeStruct((rows, rank), F32),
        compiler_params=pltpu.CompilerParams(vmem_limit_bytes=VMEM_LIMIT_BYTES),
        name="cond",
    )(c_pad, w_cond, b_cond.reshape(1, rank))
    tn = _tile(n6, 2048)
    mod = pl.pallas_call(
        _mod_kernel,
        out_shape=jax.ShapeDtypeStruct((depth, rows, n6), F32),
        grid=(depth, n6 // tn),
        in_specs=[
            pl.BlockSpec((rows, rank), lambda l, j: (0, 0)),
            pl.BlockSpec((1, rank, tn), lambda l, j: (l, 0, j)),
            pl.BlockSpec((1, 1, tn), lambda l, j: (l, 0, j)),
        ],
        out_specs=pl.BlockSpec((1, rows, tn), lambda l, j: (l, 0, j)),
        compiler_params=_params("parallel", "parallel"),
        name="mod",
    )(cond, w_mod, b_mod.reshape(depth, 1, n6))
    return mod[:, :bsz].reshape(depth, bsz, n6 // d, 1, d)


def _cast_kernel(w_ref, o_ref):
    o_ref[...] = w_ref[0].astype(o_ref.dtype)


def _bf16_weight(w, index):
    _, rows, cols = w.shape
    tr = _tile(rows, 512)
    return pl.pallas_call(
        _cast_kernel,
        out_shape=jax.ShapeDtypeStruct((rows, cols), BF16),
        grid=(rows // tr,),
        in_specs=[pl.BlockSpec((1, tr, cols), lambda i: (index, i, 0))],
        out_specs=pl.BlockSpec((tr, cols), lambda i: (i, 0)),
        compiler_params=_params("parallel"),
        name="weight_to_bf16",
    )(w)


def _norm_mm_kernel(x_ref, g_ref, sc_ref, sh_ref, w_ref, b_ref, o_ref, h_ref, *, gelu):
    @pl.when(pl.program_id(1) == 0)
    def _():
        h_ref[...] = _modulated_rms_norm(x_ref[...], g_ref[...], sc_ref[0], sh_ref[0]).astype(BF16)

    acc = jnp.dot(h_ref[...], w_ref[...], preferred_element_type=F32) + b_ref[...]
    if gelu:
        acc = 0.5 * acc * (1.0 + lax.erf(acc * F32(0.7071067811865476)))
    o_ref[...] = acc.astype(o_ref.dtype)


def _norm_matmul(x, g, sc, sh, w, b, *, seq_len, gelu):
    t, d = x.shape
    n = w.shape[1]
    tm = _tile(seq_len, 512)
    tn = _tile(n, 1024)
    per_seq = seq_len // tm
    return pl.pallas_call(
        functools.partial(_norm_mm_kernel, gelu=gelu),
        out_shape=jax.ShapeDtypeStruct((t, n), BF16),
        grid=(t // tm, n // tn),
        in_specs=[
            pl.BlockSpec((tm, d), lambda i, j: (i, 0)),
            pl.BlockSpec((1, d), lambda i, j: (0, 0)),
            pl.BlockSpec((1, 1, d), lambda i, j: (i // per_seq, 0, 0)),
            pl.BlockSpec((1, 1, d), lambda i, j: (i // per_seq, 0, 0)),
            pl.BlockSpec((d, tn), lambda i, j: (0, j)),
            pl.BlockSpec((1, tn), lambda i, j: (0, j)),
        ],
        out_specs=pl.BlockSpec((tm, tn), lambda i, j: (i, j)),
        scratch_shapes=[pltpu.VMEM((tm, d), BF16)],
        compiler_params=_params("parallel", "arbitrary"),
        name="norm_matmul_gelu" if gelu else "norm_matmul",
    )(x, g, sc, sh, w, b)


def _mm_residual_kernel(a_ref, w_ref, x_ref, gate_ref, o_ref):
    y = jnp.dot(a_ref[...], w_ref[...], preferred_element_type=F32)
    o_ref[...] = x_ref[...] + gate_ref[0] * y


def _matmul_residual(a, w, x, gate, *, seq_len):
    t, k = a.shape
    d = w.shape[1]
    tm = _tile(seq_len, 1024)
    tn = _tile(d, 512)
    per_seq = seq_len // tm
    return pl.pallas_call(
        _mm_residual_kernel,
        out_shape=jax.ShapeDtypeStruct((t, d), F32),
        grid=(t // tm, d // tn),
        in_specs=[
            pl.BlockSpec((tm, k), lambda i, j: (i, 0)),
            pl.BlockSpec((k, tn), lambda i, j: (0, j)),
            pl.BlockSpec((tm, tn), lambda i, j: (i, j)),
            pl.BlockSpec((1, 1, tn), lambda i, j: (i // per_seq, 0, j)),
        ],
        out_specs=pl.BlockSpec((tm, tn), lambda i, j: (i, j)),
        compiler_params=_params("parallel", "parallel"),
        name="matmul_residual",
    )(a, w, x, gate)


def _pool_kernel(u_ref, halo_ref, wg_ref, scale_ref, o_ref, ext_ref, ping_ref, pong_ref, *, seq_len):
    tm = u_ref.shape[0]
    gw = wg_ref.shape[1]
    start = lax.rem(pl.program_id(0) * tm, seq_len)
    ext_ref[0:POOL_HALO] = jnp.where(start == 0, 0.0, halo_ref[...].astype(F32))
    ext_ref[POOL_HALO:] = u_ref[...].astype(F32)
    pos = start + lax.broadcasted_iota(I32, (tm, 1), 0)
    for g, window in enumerate(POOL_WINDOWS):
        cols = slice(g * gw, (g + 1) * gw)
        cur = ext_ref[POOL_HALO:POOL_HALO + tm, cols]
        read = lambda first, n: ext_ref[POOL_HALO + first:POOL_HALO + first + n, cols]
        spare = [ping_ref, pong_ref]
        w = 1
        while True:
            history = window - 2 * w
            total = read(-history, tm + history) + read(-history - w, tm + history)
            w *= 2
            if w == window:
                break
            held = spare[0]
            spare.reverse()
            held[0:tm + history, :] = total
            read = lambda first, n, held=held, history=history: held[history + first:history + first + n, :]
        inv_cnt = 1.0 / jnp.minimum(pos + 1, window).astype(F32)
        pooled = total * inv_cnt - cur
        mixed = jnp.dot(pooled.astype(BF16), wg_ref[g], preferred_element_type=F32)
        o_ref[:, cols] = (mixed * scale_ref[:, cols]).astype(o_ref.dtype)


def _pool_middle(u, w_group, scale, *, seq_len):
    t, d = u.shape
    assert w_group.shape[0] == len(POOL_WINDOWS) and max(POOL_WINDOWS) <= POOL_HALO
    tm = _tile(seq_len, 512)
    halo_blocks = tm // POOL_HALO
    return pl.pallas_call(
        functools.partial(_pool_kernel, seq_len=seq_len),
        out_shape=jax.ShapeDtypeStruct((t, d), BF16),
        grid=(t // tm,),
        in_specs=[
            pl.BlockSpec((tm, d), lambda i: (i, 0)),
            pl.BlockSpec((POOL_HALO, d), lambda i: (jnp.maximum(i * halo_blocks - 1, 0), 0)),
            pl.BlockSpec(w_group.shape, lambda i: (0, 0, 0)),
            pl.BlockSpec((1, d), lambda i: (0, 0)),
        ],
        out_specs=pl.BlockSpec((tm, d), lambda i: (i, 0)),
        scratch_shapes=[pltpu.VMEM((tm + POOL_HALO, d), F32),
                        pltpu.VMEM((tm + POOL_HALO, d // len(POOL_WINDOWS)), F32),
                        pltpu.VMEM((tm + POOL_HALO, d // len(POOL_WINDOWS)), F32)],
        compiler_params=_params("parallel"),
        name="pool_middle",
    )(u, u, w_group, scale)


def _sgu_kernel(u_ref, v_ref, lng_ref, lnb_ref, ws_ref, bst_ref, o_ref, vn_ref):
    tm, width = u_ref.shape
    n_heads, chunk, _ = ws_ref.shape
    hw = width // n_heads
    v = v_ref[...].astype(F32)
    vc = v - jnp.mean(v, axis=-1, keepdims=True)
    vn = vc * lax.rsqrt(jnp.mean(vc * vc, axis=-1, keepdims=True) + NORM_EPS)
    vn_ref[...] = (vn * lng_ref[...] + lnb_ref[...]).astype(BF16)
    causal = (lax.broadcasted_iota(I32, (chunk, chunk), 0) >= lax.broadcasted_iota(I32, (chunk, chunk), 1))
    for h in range(n_heads):
        w_causal = jnp.where(causal, ws_ref[h], 0.0).astype(BF16)
        bias = bst_ref[:, h:h + 1]
        cols = slice(h * hw, (h + 1) * hw)
        for c in range(tm // chunk):
            rows = slice(c * chunk, (c + 1) * chunk)
            mixed = jnp.dot(w_causal, vn_ref[rows, cols], preferred_element_type=F32) + bias
            o_ref[rows, cols] = (u_ref[rows, cols].astype(F32) * mixed).astype(o_ref.dtype)


def _sgu_middle(uv, ln_g, ln_b, w_s, b_s_t, *, seq_len):
    t, two_w = uv.shape
    width = two_w // 2
    chunk = w_s.shape[1]
    tm = _tile(seq_len, 4 * chunk)
    return pl.pallas_call(
        _sgu_kernel,
        out_shape=jax.ShapeDtypeStruct((t, width), BF16),
        grid=(t // tm,),
        in_specs=[
            pl.BlockSpec((tm, width), lambda i: (i, 0)),
            pl.BlockSpec((tm, width), lambda i: (i, 1)),
            pl.BlockSpec((1, width), lambda i: (0, 0)),
            pl.BlockSpec((1, width), lambda i: (0, 0)),
            pl.BlockSpec(w_s.shape, lambda i: (0, 0, 0)),
            pl.BlockSpec(b_s_t.shape, lambda i: (0, 0)),
        ],
        out_specs=pl.BlockSpec((tm, width), lambda i: (i, 0)),
        scratch_shapes=[pltpu.VMEM((tm, width), BF16)],
        compiler_params=_params("parallel"),
        name="sgu_middle",
    )(uv, uv, ln_g, ln_b, w_s, b_s_t)


def _router_kernel(x_ref, g_ref, sc_ref, sh_ref, wr_ref, br_ref,
                   hp_ref, gates_ref, eid_ref, rank_ref, counts_ref, carry_ref, *, n_experts):
    tm = x_ref.shape[0]

    @pl.when(pl.program_id(0) == 0)
    def _():
        carry_ref[...] = jnp.zeros_like(carry_ref)

    h = _modulated_rms_norm(x_ref[...], g_ref[...], sc_ref[0], sh_ref[0])
    hp_ref[...] = _pack_halves(h)

    lane = lax.broadcasted_iota(I32, (tm, LANES), 1)
    w_lane = lax.broadcasted_iota(I32, wr_ref.shape, 1)
    w = wr_ref[...]
    w_hi = w.astype(BF16)
    w_lo = (w - w_hi.astype(F32)).astype(BF16)
    h_hi = h.astype(BF16)
    h_lo = (h - h_hi.astype(F32)).astype(BF16)
    first = jnp.dot(h_hi, jnp.where(w_lane < n_experts, w_hi, w_lo), preferred_element_type=F32)
    second = jnp.dot(h_lo, w_hi, preferred_element_type=F32)
    logits = first + second + pltpu.roll(first, LANES - n_experts, 1) + br_ref[...]
    remaining = jnp.where(lane < n_experts, logits, -jnp.inf)
    selected = jnp.zeros((tm, LANES), F32)
    top_vals, top_ids = [], []
    for _ in range(TOP_K):
        best = jnp.max(remaining, axis=-1, keepdims=True)
        idx = jnp.min(jnp.where(remaining == best, lane, LANES), axis=-1, keepdims=True)
        hit = lane == idx
        selected = jnp.where(hit, 1.0, selected)
        remaining = jnp.where(hit, -jnp.inf, remaining)
        top_vals.append(best)
        top_ids.append(idx)
    exps = [jnp.exp(v - top_vals[0]) for v in top_vals]
    denom = exps[0]
    for e in exps[1:]:
        denom = denom + e

    lower = (lax.broadcasted_iota(I32, (tm, tm), 0) >= lax.broadcasted_iota(I32, (tm, tm), 1))
    incl = jnp.dot(lower.astype(BF16), selected.astype(BF16), preferred_element_type=F32)
    before = incl - selected + carry_ref[...]
    gates = jnp.zeros((tm, LANES), F32)
    eids = jnp.zeros((tm, LANES), I32)
    ranks = jnp.zeros((tm, LANES), I32)
    for k in range(TOP_K):
        rank_k = jnp.sum(jnp.where(lane == top_ids[k], before, 0.0), axis=-1, keepdims=True)
        gates = jnp.where(lane == k, exps[k] / denom, gates)
        eids = jnp.where(lane == k, top_ids[k], eids)
        ranks = jnp.where(lane == k, rank_k.astype(I32), ranks)
    gates_ref[...] = gates
    eid_ref[...] = eids
    rank_ref[...] = ranks
    carry_ref[...] = carry_ref[...] + incl[tm - 1:tm, :]
    counts_ref[...] = carry_ref[...].astype(I32)


def _route(x, g, sc, sh, w_router_pad, b_router_pad, *, seq_len, n_experts):
    t, d = x.shape
    tm = _tile(seq_len, 512)
    per_seq = seq_len // tm
    tok_spec = pl.BlockSpec((tm, LANES), lambda i: (i, 0))
    return pl.pallas_call(
        functools.partial(_router_kernel, n_experts=n_experts),
        out_shape=[
            jax.ShapeDtypeStruct((t, d // 2), U32),
            jax.ShapeDtypeStruct((t, LANES), F32),
            jax.ShapeDtypeStruct((t, LANES), I32),
            jax.ShapeDtypeStruct((t, LANES), I32),
            jax.ShapeDtypeStruct((1, LANES), I32),
        ],
        grid=(t // tm,),
        in_specs=[
            pl.BlockSpec((tm, d), lambda i: (i, 0)),
            pl.BlockSpec((1, d), lambda i: (0, 0)),
            pl.BlockSpec((1, 1, d), lambda i: (i // per_seq, 0, 0)),
            pl.BlockSpec((1, 1, d), lambda i: (i // per_seq, 0, 0)),
            pl.BlockSpec((d, LANES), lambda i: (0, 0)),
            pl.BlockSpec((1, LANES), lambda i: (0, 0)),
        ],
        out_specs=[
            pl.BlockSpec((tm, d // 2), lambda i: (i, 0)),
            tok_spec, tok_spec, tok_spec,
            pl.BlockSpec((1, LANES), lambda i: (0, 0)),
        ],
        scratch_shapes=[pltpu.VMEM((1, LANES), F32)],
        compiler_params=_params("arbitrary"),
        name="moe_router",
    )(x, g, sc, sh, w_router_pad, b_router_pad)


def _dispatch_kernel(dest_ref, pad_lo_ref, pad_hi_ref, hp_ref, xs_ref,
                     stage_ref, zero_ref, stage_sems, row_sems, pad_sem):
    n_slots, tm, _ = stage_ref.shape
    step = pl.program_id(0)
    n_steps = pl.num_programs(0)
    slot = lax.rem(step, n_slots)
    parity = lax.rem(step, 2)

    def stage_copy(tile, into):
        rows = pl.ds(pl.multiple_of(tile * tm, tm), tm)
        return pltpu.make_async_copy(hp_ref.at[rows], stage_ref.at[into], stage_sems.at[into])

    @pl.when(step == 0)
    def _():
        stage_copy(0, 0).start()
        zero_ref[...] = jnp.zeros_like(zero_ref)
        chunk = zero_ref.shape[0]

        def chunk_copy(hi, i):
            start = pl.multiple_of(hi - (i + 1) * chunk, chunk)
            return pltpu.make_async_copy(zero_ref, xs_ref.at[pl.ds(start, chunk)], pad_sem)

        def row_copy(r):
            return pltpu.make_async_copy(zero_ref.at[pl.ds(0, 1)], xs_ref.at[pl.ds(r, 1)], pad_sem)

        def for_each_copy(act):
            for e in range(pad_lo_ref.shape[0]):
                lo, hi = pad_lo_ref[e], pad_hi_ref[e]
                n_chunks = (hi - lo) // chunk
                lax.fori_loop(0, n_chunks, lambda i, c: (act(chunk_copy(hi, i)), c)[1], 0)
                lax.fori_loop(lo, hi - n_chunks * chunk, lambda r, c: (act(row_copy(r)), c)[1], 0)

        for_each_copy(lambda cp: cp.start())
        for_each_copy(lambda cp: cp.wait())

    stage_copy(step, slot).wait()

    @pl.when(step + 1 < n_steps)
    def _():
        stage_copy(step + 1, lax.rem(step + 1, n_slots)).start()

    def issue(t, carry):
        for k in range(TOP_K):
            d = dest_ref[(step * tm + t) * TOP_K + k]
            pltpu.make_async_copy(stage_ref.at[slot, pl.ds(t, 1)], xs_ref.at[pl.ds(d, 1)],
                                  row_sems.at[parity]).start(priority=k % 2)
        return carry

    lax.fori_loop(0, tm, issue, 0, unroll=DMA_ISSUE_UNROLL)

    def wait_rows(which):
        for _ in range(TOP_K):
            pltpu.make_async_copy(stage_ref.at[0], xs_ref.at[pl.ds(0, tm)], row_sems.at[which]).wait()

    @pl.when(step > 0)
    def _():
        wait_rows(1 - parity)

    @pl.when(step == n_steps - 1)
    def _():
        wait_rows(parity)


def _dispatch(dest, pad_lo, pad_hi, hp, n_rows):
    t, half = hp.shape
    tm = _tile(t, 256)
    return pl.pallas_call(
        _dispatch_kernel,
        out_shape=jax.ShapeDtypeStruct((n_rows, half), U32),
        grid_spec=pltpu.PrefetchScalarGridSpec(
            num_scalar_prefetch=3,
            grid=(t // tm,),
            in_specs=[pl.BlockSpec(memory_space=pl.ANY)],
            out_specs=pl.BlockSpec(memory_space=pl.ANY),
            scratch_shapes=[pltpu.VMEM((DISPATCH_SLOTS, tm, half), U32),
                            pltpu.VMEM((ZERO_CHUNK_ROWS, half), U32),
                            pltpu.SemaphoreType.DMA((DISPATCH_SLOTS,)),
                            pltpu.SemaphoreType.DMA((2,)), pltpu.SemaphoreType.DMA(())],
        ),
        compiler_params=_params("arbitrary"),
        name="moe_dispatch",
    )(dest, pad_lo, pad_hi, hp)


def _expert_kernel(blk_expert_ref, next_expert_ref, n_used_ref, xs_ref, wu_hbm, bu_ref, wd_hbm, bd_ref, y_ref,
                   wu_f32, wd_f32, wu_bf, wd_bf, xb_ref, sems, *, layer):
    half = xs_ref.shape[1]
    f = wd_bf.shape[0]
    b = pl.program_id(0)
    live = b < n_used_ref[0]
    expert = blk_expert_ref[b]
    opens_expert = jnp.logical_or(b == 0, blk_expert_ref[jnp.maximum(b - 1, 0)] != expert)

    def weight_copies(e):
        return (pltpu.make_async_copy(wu_hbm.at[layer, e], wu_f32, sems.at[0]),
                pltpu.make_async_copy(wd_hbm.at[layer, e], wd_f32, sems.at[1]))

    @pl.when(jnp.logical_and(live, opens_expert))
    def _():
        @pl.when(b == 0)
        def _():
            for cp in weight_copies(expert):
                cp.start()

        for cp in weight_copies(expert):
            cp.wait()
        wu_bf[...] = wu_f32[...].astype(BF16)
        wd_bf[...] = wd_f32[...].astype(BF16)
        upcoming = next_expert_ref[b]

        @pl.when(upcoming >= 0)
        def _():
            for cp in weight_copies(upcoming):
                cp.start()

    @pl.when(live)
    def _():
        lo, hi = _unpack_halves(xs_ref[...])
        xb_ref[:, :half] = lo.astype(BF16)
        xb_ref[:, half:] = hi.astype(BF16)
        rows_all = xs_ref.shape[0]
        sub = rows_all // 2
        wide = 2 * MXU_COLS
        row_halves = [slice(s * sub, (s + 1) * sub) for s in range(2)]
        wide = min(wide, wu_bf.shape[1])
        hu_parts = [[jnp.dot(xb_ref[rows, :], wu_bf[:, :wide], preferred_element_type=F32)] for rows in row_halves]
        if wide < wu_bf.shape[1]:
            for s, rows in enumerate(row_halves):
                hu_parts[s].append(jnp.dot(xb_ref[rows, :], wu_bf[:, wide:], preferred_element_type=F32))
        for s, rows in enumerate(row_halves):
            hu = jnp.concatenate(hu_parts[s], axis=1) + bu_ref[0]
            glu = jnp.minimum(hu[:, :f], SWIGLU_LIMIT)
            lin = jnp.clip(hu[:, f:], -SWIGLU_LIMIT, SWIGLU_LIMIT)
            act = (glu * jax.nn.sigmoid(SWIGLU_ALPHA * glu) * (lin + 1.0)).astype(BF16)
            cw = _tile(half, 512)
            for c in range(half // cw):
                pair = []
                for cols in (slice(c * cw, (c + 1) * cw), slice(half + c * cw, half + (c + 1) * cw)):
                    y = jnp.dot(act, wd_bf[:, cols], preferred_element_type=F32) + bd_ref[0, :, cols]
                    pair.append(lax.bitcast_convert_type(y.astype(BF16).astype(F32), U32))
                y_ref[rows, c * cw:(c + 1) * cw] = (pair[0] >> 16) | (pair[1] & U32(HI_MASK))

    @pl.when(jnp.logical_not(live))
    def _():
        y_ref[...] = jnp.zeros_like(y_ref)


def _experts(blk_expert, next_expert, n_used, xs, w_up, b_up, w_down, b_down, *, layer, block_rows):
    n_rows, half = xs.shape
    _, n_experts, d, f2 = w_up.shape
    f = f2 // 2
    n_blocks = n_rows // block_rows

    def row_block(b, blk_expert, next_expert, n_used):
        return (jnp.minimum(b, n_used[0] - 1), 0)

    def expert_block(b, blk_expert, next_expert, n_used):
        return (blk_expert[b], 0, 0)

    return pl.pallas_call(
        functools.partial(_expert_kernel, layer=layer),
        out_shape=jax.ShapeDtypeStruct((n_rows, half), U32),
        grid_spec=pltpu.PrefetchScalarGridSpec(
            num_scalar_prefetch=3,
            grid=(n_blocks,),
            in_specs=[
                pl.BlockSpec((block_rows, half), row_block),
                pl.BlockSpec(memory_space=pl.ANY),
                pl.BlockSpec((1, 1, f2), expert_block),
                pl.BlockSpec(memory_space=pl.ANY),
                pl.BlockSpec((1, 1, d), expert_block),
            ],
            out_specs=pl.BlockSpec((block_rows, half), lambda b, *_: (b, 0)),
            scratch_shapes=[
                pltpu.VMEM((d, f2), F32), pltpu.VMEM((f, d), F32),
                pltpu.VMEM((d, f2), BF16), pltpu.VMEM((f, d), BF16),
                pltpu.VMEM((block_rows, d), BF16),
                pltpu.SemaphoreType.DMA((2,)),
            ],
        ),
        compiler_params=_params("arbitrary"),
        name="moe_experts",
    )(blk_expert, next_expert, n_used, xs, w_up, b_up.reshape(n_experts, 1, f2),
      w_down, b_down.reshape(n_experts, 1, d))


def _combine_kernel(dest_ref, yb_ref, x_ref, gates_ref, gt_ref, o_ref, buf_ref, sems):
    tm = x_ref.shape[0]
    half = buf_ref.shape[3]
    step = pl.program_id(0)
    slot = lax.rem(step, 2)

    n_groups = tm // SUBLANES
    half_groups = n_groups // 2
    half_rows = tm // 2

    def request_rows(tile, into, group, part):
        for j in range(SUBLANES):
            t = group * SUBLANES + j
            for k in range(TOP_K):
                d = dest_ref[(tile * tm + t) * TOP_K + k]
                pltpu.make_async_copy(yb_ref.at[pl.ds(d, 1)], buf_ref.at[into, k, pl.ds(t, 1)],
                                      sems.at[into, part]).start()

    def wait_rows(part):
        for k in range(TOP_K):
            pltpu.make_async_copy(yb_ref.at[pl.ds(0, half_rows)], buf_ref.at[slot, k, pl.ds(0, half_rows)],
                                  sems.at[slot, part]).wait()

    def weighted_sum(group):
        rows = pl.ds(pl.multiple_of(group * SUBLANES, SUBLANES), SUBLANES)
        acc_lo = acc_hi = None
        for k in range(TOP_K):
            lo, hi = _unpack_halves(buf_ref[slot, k, rows, :])
            gate = gates_ref[rows, k:k + 1]
            acc_lo = gate * lo if acc_lo is None else acc_lo + gate * lo
            acc_hi = gate * hi if acc_hi is None else acc_hi + gate * hi
        o_ref[rows, :half] = x_ref[rows, :half] + gt_ref[0, :, :half] * acc_lo
        o_ref[rows, half:] = x_ref[rows, half:] + gt_ref[0, :, half:] * acc_hi

    @pl.when(step == 0)
    def _():
        for part in range(2):
            lax.fori_loop(part * half_groups, (part + 1) * half_groups,
                          lambda g, c, part=part: (request_rows(0, 0, g, part), c)[1], 0)

    has_next = step + 1 < pl.num_programs(0)
    for part in range(2):
        wait_rows(part)
        first, last = part * half_groups, (part + 1) * half_groups

        @pl.when(has_next)
        def _():
            def body(g, c):
                request_rows(step + 1, 1 - slot, g, part)
                weighted_sum(g)
                return c

            lax.fori_loop(first, last, body, 0)

        @pl.when(jnp.logical_not(has_next))
        def _():
            lax.fori_loop(first, last, lambda g, c: (weighted_sum(g), c)[1], 0)


def _combine(dest, yb, x, gates, gt, *, seq_len):
    t, d = x.shape
    half = d // 2
    tm = _tile(seq_len, 256)
    per_seq = seq_len // tm
    return pl.pallas_call(
        _combine_kernel,
        out_shape=jax.ShapeDtypeStruct((t, d), F32),
        grid_spec=pltpu.PrefetchScalarGridSpec(
            num_scalar_prefetch=1,
            grid=(t // tm,),
            in_specs=[
                pl.BlockSpec(memory_space=pl.ANY),
                pl.BlockSpec((tm, d), lambda i, dest: (i, 0)),
                pl.BlockSpec((tm, LANES), lambda i, dest: (i, 0)),
                pl.BlockSpec((1, 1, d), lambda i, dest: (i // per_seq, 0, 0)),
            ],
            out_specs=pl.BlockSpec((tm, d), lambda i, dest: (i, 0)),
            scratch_shapes=[pltpu.VMEM((2, TOP_K, tm, half), U32), pltpu.SemaphoreType.DMA((2, 2))],
        ),
        input_output_aliases={2: 0},
        compiler_params=_params("arbitrary"),
        name="moe_combine",
    )(dest, yb, x, gates, gt)


def _moe_layer(x, g, sc, sh, gt, w_router, b_router, w_up, b_up, w_down, b_down, *, layer, seq_len, block_rows):
    t, d = x.shape
    n_experts = w_router.shape[1]
    assert 2 * n_experts <= LANES
    w_router_pad = jnp.pad(jnp.concatenate([w_router, w_router], axis=1), ((0, 0), (0, LANES - 2 * n_experts)))
    b_router_pad = jnp.pad(b_router, (0, LANES - n_experts)).reshape(1, LANES)
    hp, gates, eids, ranks, counts = _route(x, g, sc, sh, w_router_pad, b_router_pad,
                                            seq_len=seq_len, n_experts=n_experts)
    counts = counts[0, :n_experts]
    padded = (counts + block_rows - 1) // block_rows * block_rows
    seg_end = jnp.cumsum(padded)
    seg_start = seg_end - padded
    n_rows = t * TOP_K + n_experts * block_rows
    n_blocks = n_rows // block_rows
    expert_ids = jnp.arange(n_experts, dtype=I32)
    is_expert = eids[:, :TOP_K, None] == expert_ids
    dest = (ranks[:, :TOP_K] + jnp.sum(jnp.where(is_expert, seg_start, 0), axis=-1)).reshape(-1).astype(I32)
    blk_start = jnp.arange(n_blocks, dtype=I32) * block_rows
    blk_expert = jnp.minimum(jnp.sum(blk_start[:, None] >= seg_end[None, :], axis=1), n_experts - 1).astype(I32)
    n_used = (seg_end[-1:] // block_rows).astype(I32)
    later = lax.cummin(jnp.where(counts > 0, expert_ids, n_experts), axis=0, reverse=True)
    following = jnp.concatenate([later[1:], jnp.full((1,), n_experts, I32)])
    next_expert = jnp.where(following < n_experts, following, -1)[blk_expert].astype(I32)

    assert block_rows % ZERO_CHUNK_ROWS == 0
    pad_lo = jnp.concatenate([seg_start + counts, seg_end[-1:]]).astype(I32)
    pad_hi = jnp.concatenate([seg_end, jnp.full((1,), n_rows, I32)]).astype(I32)
    xs = _dispatch(dest, pad_lo, pad_hi, hp, n_rows)
    yb = _experts(blk_expert, next_expert, n_used, xs, w_up, b_up, w_down, b_down,
                  layer=layer, block_rows=block_rows)
    return _combine(dest, yb, x, gates, gt, seq_len=seq_len)


def _final_norm_kernel(x_ref, g_ref, o_ref):
    x = x_ref[...]
    o_ref[...] = x * lax.rsqrt(jnp.mean(x * x, axis=-1, keepdims=True) + NORM_EPS) * g_ref[...]


def _final_norm(x, g):
    t, d = x.shape
    tm = _tile(t, 512)
    return pl.pallas_call(
        _final_norm_kernel,
        out_shape=jax.ShapeDtypeStruct((t, d), F32),
        grid=(t // tm,),
        in_specs=[pl.BlockSpec((tm, d), lambda i: (i, 0)), pl.BlockSpec((1, d), lambda i: (0, 0))],
        out_specs=pl.BlockSpec((tm, d), lambda i: (i, 0)),
        compiler_params=_params("parallel"),
        name="final_norm",
    )(x, g)


def kernel(x, c, w_cond, b_cond, w_mod, b_mod, g_norm_mix, g_norm_ffn, pool_w_in, pool_w_group, pool_scale, pool_w_out, gmlp_w_in, gmlp_b_in, gmlp_ln_g, gmlp_ln_b, gmlp_w_s, gmlp_b_s, gmlp_w_out, moe_w_router, moe_b_router, moe_w_up, moe_b_up, moe_w_down, moe_b_down, g_final):
    bsz, seq_len, d = x.shape
    depth = w_mod.shape[0]
    mod = _modulation(c, w_cond, b_cond, w_mod, b_mod)
    xt = x.reshape(bsz * seq_len, d)
    zero_bias = jnp.zeros((1, d), F32)
    for layer in range(depth):
        sh1, sc1, gt1, sh2, sc2, gt2 = (mod[layer, :, m] for m in range(6))
        j = layer // 2
        g_mix = g_norm_mix[layer].reshape(1, d)
        if layer % 2 == 0:
            u = _norm_matmul(xt, g_mix, sc1, sh1, _bf16_weight(pool_w_in, j), zero_bias,
                             seq_len=seq_len, gelu=False)
            n_groups, gw, _ = pool_w_group.shape[1:]
            w_group = _bf16_weight(pool_w_group.reshape(-1, n_groups * gw, gw), j).reshape(n_groups, gw, gw)
            mid = _pool_middle(u, w_group, pool_scale[j].reshape(1, d), seq_len=seq_len)
            w_out = _bf16_weight(pool_w_out, j)
        else:
            uv = _norm_matmul(xt, g_mix, sc1, sh1, _bf16_weight(gmlp_w_in, j), gmlp_b_in[j].reshape(1, -1),
                              seq_len=seq_len, gelu=True)
            mid = _sgu_middle(uv, gmlp_ln_g[j].reshape(1, -1), gmlp_ln_b[j].reshape(1, -1),
                              gmlp_w_s[j], gmlp_b_s[j].T, seq_len=seq_len)
            w_out = _bf16_weight(gmlp_w_out, j)
        xt = _matmul_residual(mid, w_out, xt, gt1, seq_len=seq_len)
        xt = _moe_layer(xt, g_norm_ffn[layer].reshape(1, d), sc2, sh2, gt2,
                        moe_w_router[layer], moe_b_router[layer],
                        moe_w_up, moe_b_up[layer], moe_w_down, moe_b_down[layer],
                        layer=layer, seq_len=seq_len, block_rows=512)
    return _final_norm(xt, g_final.reshape(1, d)).reshape(bsz, seq_len, d)
```

```python
import functools

import jax
import jax.numpy as jnp
from jax import lax
from jax.experimental import pallas as pl
from jax.experimental.pallas import tpu as pltpu

F32 = jnp.float32
BF16 = jnp.bfloat16
U32 = jnp.uint32
I32 = jnp.int32
HIGHEST = lax.Precision.HIGHEST

NORM_EPS = 1e-5
POOL_WINDOWS = (2, 4, 8, 16)
TOP_K = 4
SWIGLU_LIMIT = 7.0
SWIGLU_ALPHA = 1.702

LANES = 128
SUBLANES = 8
MXU_COLS = 256
POOL_HALO = 16
NORM_GROUP_ROWS = 16
ZERO_CHUNK_ROWS = 64
DISPATCH_SLOTS = 3
DMA_ISSUE_UNROLL = 8
VMEM_LIMIT_BYTES = 56 * 1024 * 1024
HI_MASK = 0xFFFF0000


def _tile(n, preferred):
    if n <= preferred:
        return n
    t = preferred - preferred % LANES
    while n % t:
        t -= LANES
    assert t > 0, (n, preferred)
    return t


def _params(*semantics):
    return pltpu.CompilerParams(dimension_semantics=semantics, vmem_limit_bytes=VMEM_LIMIT_BYTES)


def _modulated_rms_norm(x, g, sc, sh):
    y = x * lax.rsqrt(jnp.mean(x * x, axis=-1, keepdims=True) + NORM_EPS)
    return (y * g) * (1.0 + sc) + sh


def _set_modulation_rows(mod_ref, g_ref, sc_ref, sh_ref):
    d = g_ref.shape[1]
    mod_ref[0] = jnp.broadcast_to(g_ref[...], (SUBLANES, d))
    mod_ref[1] = jnp.broadcast_to(1.0 + sc_ref[0], (SUBLANES, d))
    mod_ref[2] = jnp.broadcast_to(sh_ref[0], (SUBLANES, d))


def _modulated_rms_norm_rows(x, mod_ref):
    y = x * lax.rsqrt(jnp.mean(x * x, axis=-1, keepdims=True) + NORM_EPS)
    parts = [(y[r:r + SUBLANES] * mod_ref[0]) * mod_ref[1] + mod_ref[2] for r in range(0, x.shape[0], SUBLANES)]
    return jnp.concatenate(parts, axis=0)


def _pack_halves(v):
    half = v.shape[1] // 2
    bits = lax.bitcast_convert_type(v.astype(BF16).astype(F32), U32)
    return (bits[:, :half] >> 16) | (bits[:, half:] & U32(HI_MASK))


def _unpack_halves(w):
    lo = lax.bitcast_convert_type(w << 16, F32)
    hi = lax.bitcast_convert_type(w & U32(HI_MASK), F32)
    return lo, hi


def _cond_kernel(c_ref, w_ref, b_ref, o_ref):
    z = jnp.dot(c_ref[...], w_ref[...], precision=HIGHEST, preferred_element_type=F32) + b_ref[...]
    o_ref[...] = z * jax.nn.sigmoid(z)


def _mod_kernel(cond_ref, w_ref, b_ref, o_ref):
    o_ref[0] = jnp.dot(cond_ref[...], w_ref[0], precision=HIGHEST, preferred_element_type=F32) + b_ref[0]


def _modulation(c, w_cond, b_cond, w_mod, b_mod):
    bsz, d = c.shape
    rank = w_cond.shape[1]
    depth, _, n6 = w_mod.shape
    rows = -(-bsz // SUBLANES) * SUBLANES
    c_pad = jnp.pad(c, ((0, rows - bsz), (0, 0)))
    cond = pl.pallas_call(
        _cond_kernel,
        out_shape=jax.ShapeDtypeStruct((rows, rank), F32),
        compiler_params=pltpu.CompilerParams(vmem_limit_bytes=VMEM_LIMIT_BYTES),
        name="cond",
    )(c_pad, w_cond, b_cond.reshape(1, rank))
    tn = _tile(n6, 2048)
    mod = pl.pallas_call(
        _mod_kernel,
        out_shape=jax.ShapeDtypeStruct((depth, rows, n6), F32),
        grid=(depth, n6 // tn),
        in_specs=[
            pl.BlockSpec((rows, rank), lambda l, j: (0, 0)),
            pl.BlockSpec((1, rank, tn), lambda l, j: (l, 0, j)),
            pl.BlockSpec((1, 1, tn), lambda l, j: (l, 0, j)),
        ],
        out_specs=pl.BlockSpec((1, rows, tn), lambda l, j: (l, 0, j)),
        compiler_params=_params("parallel", "parallel"),
        name="mod",
    )(cond, w_mod, b_mod.reshape(depth, 1, n6))
    return mod[:, :bsz].reshape(depth, bsz, n6 // d, 1, d)


def _cast_kernel(w_ref, o_ref):
    o_ref[...] = w_ref[0].astype(o_ref.dtype)


def _bf16_weight(w, index):
    _, rows, cols = w.shape
    tr = _tile(rows, 512)
    return pl.pallas_call(
        _cast_kernel,
        out_shape=jax.ShapeDtypeStruct((rows, cols), BF16),
        grid=(rows // tr,),
        in_specs=[pl.BlockSpec((1, tr, cols), lambda i: (index, i, 0))],
        out_specs=pl.BlockSpec((tr, cols), lambda i: (i, 0)),
        compiler_params=_params("parallel"),
        name="weight_to_bf16",
    )(w)


def _norm_mm_kernel(x_ref, g_ref, sc_ref, sh_ref, w_ref, b_ref, o_ref, h_ref, mod_ref, *, gelu):
    @pl.when(pl.program_id(1) == 0)
    def _():
        _set_modulation_rows(mod_ref, g_ref, sc_ref, sh_ref)
        for r in range(0, x_ref.shape[0], NORM_GROUP_ROWS):
            rows = slice(r, r + NORM_GROUP_ROWS)
            h_ref[rows, :] = _modulated_rms_norm_rows(x_ref[rows, :], mod_ref).astype(BF16)

    acc = jnp.dot(h_ref[...], w_ref[...], preferred_element_type=F32) + b_ref[...]
    if gelu:
        acc = 0.5 * acc * (1.0 + lax.erf(acc * F32(0.7071067811865476)))
    o_ref[...] = acc.astype(o_ref.dtype)


def _norm_matmul(x, g, sc, sh, w, b, *, seq_len, gelu):
    t, d = x.shape
    n = w.shape[1]
    tm = _tile(seq_len, 512)
    tn = _tile(n, 1024)
    per_seq = seq_len // tm
    return pl.pallas_call(
        functools.partial(_norm_mm_kernel, gelu=gelu),
        out_shape=jax.ShapeDtypeStruct((t, n), BF16),
        grid=(t // tm, n // tn),
        in_specs=[
            pl.BlockSpec((tm, d), lambda i, j: (i, 0)),
            pl.BlockSpec((1, d), lambda i, j: (0, 0)),
            pl.BlockSpec((1, 1, d), lambda i, j: (i // per_seq, 0, 0)),
            pl.BlockSpec((1, 1, d), lambda i, j: (i // per_seq, 0, 0)),
            pl.BlockSpec((d, tn), lambda i, j: (0, j)),
            pl.BlockSpec((1, tn), lambda i, j: (0, j)),
        ],
        out_specs=pl.BlockSpec((tm, tn), lambda i, j: (i, j)),
        scratch_shapes=[pltpu.VMEM((tm, d), BF16), pltpu.VMEM((3, SUBLANES, d), F32)],
        compiler_params=_params("parallel", "arbitrary"),
        name="norm_matmul_gelu" if gelu else "norm_matmul",
    )(x, g, sc, sh, w, b)


def _mm_residual_kernel(a_ref, w_ref, x_ref, gate_ref, o_ref):
    y = jnp.dot(a_ref[...], w_ref[...], preferred_element_type=F32)
    o_ref[...] = x_ref[...] + gate_ref[0] * y


def _matmul_residual(a, w, x, gate, *, seq_len):
    t, k = a.shape
    d = w.shape[1]
    tm = _tile(seq_len, 1024)
    tn = _tile(d, 512)
    per_seq = seq_len // tm
    return pl.pallas_call(
        _mm_residual_kernel,
        out_shape=jax.ShapeDtypeStruct((t, d), F32),
        grid=(t // tm, d // tn),
        in_specs=[
            pl.BlockSpec((tm, k), lambda i, j: (i, 0)),
            pl.BlockSpec((k, tn), lambda i, j: (0, j)),
            pl.BlockSpec((tm, tn), lambda i, j: (i, j)),
            pl.BlockSpec((1, 1, tn), lambda i, j: (i // per_seq, 0, j)),
        ],
        out_specs=pl.BlockSpec((tm, tn), lambda i, j: (i, j)),
        compiler_params=_params("parallel", "parallel"),
        name="matmul_residual",
    )(a, w, x, gate)


def _pool_kernel(u_ref, halo_ref, wg_ref, scale_ref, o_ref, ext_ref, ping_ref, pong_ref, *, seq_len):
    tm = u_ref.shape[0]
    gw = wg_ref.shape[1]
    start = lax.rem(pl.program_id(0) * tm, seq_len)
    ext_ref[0:POOL_HALO] = jnp.where(start == 0, 0.0, halo_ref[...].astype(F32))
    ext_ref[POOL_HALO:] = u_ref[...].astype(F32)
    pos = start + lax.broadcasted_iota(I32, (tm, 1), 0)
    for g, window in enumerate(POOL_WINDOWS):
        cols = slice(g * gw, (g + 1) * gw)
        cur = ext_ref[POOL_HALO:POOL_HALO + tm, cols]
        read = lambda first, n: ext_ref[POOL_HALO + first:POOL_HALO + first + n, cols]
        spare = [ping_ref, pong_ref]
        w = 1
        while True:
            history = window - 2 * w
            total = read(-history, tm + history) + read(-history - w, tm + history)
            w *= 2
            if w == window:
                break
            held = spare[0]
            spare.reverse()
            held[0:tm + history, :] = total
            read = lambda first, n, held=held, history=history: held[history + first:history + first + n, :]
        inv_cnt = 1.0 / jnp.minimum(pos + 1, window).astype(F32)
        pooled = total * inv_cnt - cur
        mixed = jnp.dot(pooled.astype(BF16), wg_ref[g], preferred_element_type=F32)
        o_ref[:, cols] = (mixed * scale_ref[:, cols]).astype(o_ref.dtype)


def _pool_middle(u, w_group, scale, *, seq_len):
    t, d = u.shape
    assert w_group.shape[0] == len(POOL_WINDOWS) and max(POOL_WINDOWS) <= POOL_HALO
    tm = _tile(seq_len, 512)
    halo_blocks = tm // POOL_HALO
    return pl.pallas_call(
        functools.partial(_pool_kernel, seq_len=seq_len),
        out_shape=jax.ShapeDtypeStruct((t, d), BF16),
        grid=(t // tm,),
        in_specs=[
            pl.BlockSpec((tm, d), lambda i: (i, 0)),
            pl.BlockSpec((POOL_HALO, d), lambda i: (jnp.maximum(i * halo_blocks - 1, 0), 0)),
            pl.BlockSpec(w_group.shape, lambda i: (0, 0, 0)),
            pl.BlockSpec((1, d), lambda i: (0, 0)),
        ],
        out_specs=pl.BlockSpec((tm, d), lambda i: (i, 0)),
        scratch_shapes=[pltpu.VMEM((tm + POOL_HALO, d), F32),
                        pltpu.VMEM((tm + POOL_HALO, d // len(POOL_WINDOWS)), F32),
                        pltpu.VMEM((tm + POOL_HALO, d // len(POOL_WINDOWS)), F32)],
        compiler_params=_params("parallel"),
        name="pool_middle",
    )(u, u, w_group, scale)


def _sgu_kernel(u_ref, v_ref, lng_ref, lnb_ref, ws_ref, bst_ref, o_ref, vn_ref):
    tm, width = u_ref.shape
    n_heads, chunk, _ = ws_ref.shape
    hw = width // n_heads
    v = v_ref[...].astype(F32)
    vc = v - jnp.mean(v, axis=-1, keepdims=True)
    vn = vc * lax.rsqrt(jnp.mean(vc * vc, axis=-1, keepdims=True) + NORM_EPS)
    vn_ref[...] = (vn * lng_ref[...] + lnb_ref[...]).astype(BF16)
    causal = (lax.broadcasted_iota(I32, (chunk, chunk), 0) >= lax.broadcasted_iota(I32, (chunk, chunk), 1))
    for h in range(n_heads):
        w_causal = jnp.where(causal, ws_ref[h], 0.0).astype(BF16)
        bias = bst_ref[:, h:h + 1]
        cols = slice(h * hw, (h + 1) * hw)
        for c in range(tm // chunk):
            rows = slice(c * chunk, (c + 1) * chunk)
            mixed = jnp.dot(w_causal, vn_ref[rows, cols], preferred_element_type=F32) + bias
            o_ref[rows, cols] = (u_ref[rows, cols].astype(F32) * mixed).astype(o_ref.dtype)


def _sgu_middle(uv, ln_g, ln_b, w_s, b_s_t, *, seq_len):
    t, two_w = uv.shape
    width = two_w // 2
    chunk = w_s.shape[1]
    tm = _tile(seq_len, 4 * chunk)
    return pl.pallas_call(
        _sgu_kernel,
        out_shape=jax.ShapeDtypeStruct((t, width), BF16),
        grid=(t // tm,),
        in_specs=[
            pl.BlockSpec((tm, width), lambda i: (i, 0)),
            pl.BlockSpec((tm, width), lambda i: (i, 1)),
            pl.BlockSpec((1, width), lambda i: (0, 0)),
            pl.BlockSpec((1, width), lambda i: (0, 0)),
            pl.BlockSpec(w_s.shape, lambda i: (0, 0, 0)),
            pl.BlockSpec(b_s_t.shape, lambda i: (0, 0)),
        ],
        out_specs=pl.BlockSpec((tm, width), lambda i: (i, 0)),
        scratch_shapes=[pltpu.VMEM((tm, width), BF16)],
        compiler_params=_params("parallel"),
        name="sgu_middle",
    )(uv, uv, ln_g, ln_b, w_s, b_s_t)


def _router_kernel(x_ref, g_ref, sc_ref, sh_ref, wr_ref, br_ref,
                   hp_ref, gates_ref, eid_ref, rank_ref, counts_ref, carry_ref, *, n_experts):
    tm = x_ref.shape[0]

    @pl.when(pl.program_id(0) == 0)
    def _():
        carry_ref[...] = jnp.zeros_like(carry_ref)

    h = _modulated_rms_norm(x_ref[...], g_ref[...], sc_ref[0], sh_ref[0])
    hp_ref[...] = _pack_halves(h)

    lane = lax.broadcasted_iota(I32, (tm, LANES), 1)
    w_lane = lax.broadcasted_iota(I32, wr_ref.shape, 1)
    w = wr_ref[...]
    w_hi = w.astype(BF16)
    w_lo = (w - w_hi.astype(F32)).astype(BF16)
    h_hi = h.astype(BF16)
    h_lo = (h - h_hi.astype(F32)).astype(BF16)
    first = jnp.dot(h_hi, jnp.where(w_lane < n_experts, w_hi, w_lo), preferred_element_type=F32)
    second = jnp.dot(h_lo, w_hi, preferred_element_type=F32)
    logits = first + second + pltpu.roll(first, LANES - n_experts, 1) + br_ref[...]
    remaining = jnp.where(lane < n_experts, logits, -jnp.inf)
    selected = jnp.zeros((tm, LANES), F32)
    top_vals, top_ids = [], []
    for _ in range(TOP_K):
        best = jnp.max(remaining, axis=-1, keepdims=True)
        idx = jnp.min(jnp.where(remaining == best, lane, LANES), axis=-1, keepdims=True)
        hit = lane == idx
        selected = jnp.where(hit, 1.0, selected)
        remaining = jnp.where(hit, -jnp.inf, remaining)
        top_vals.append(best)
        top_ids.append(idx)
    exps = [jnp.exp(v - top_vals[0]) for v in top_vals]
    denom = exps[0]
    for e in exps[1:]:
        denom = denom + e

    lower = (lax.broadcasted_iota(I32, (tm, tm), 0) >= lax.broadcasted_iota(I32, (tm, tm), 1))
    incl = jnp.dot(lower.astype(BF16), selected.astype(BF16), preferred_element_type=F32)
    before = incl - selected + carry_ref[...]
    gates = jnp.zeros((tm, LANES), F32)
    eids = jnp.zeros((tm, LANES), I32)
    ranks = jnp.zeros((tm, LANES), I32)
    for k in range(TOP_K):
        rank_k = jnp.sum(jnp.where(lane == top_ids[k], before, 0.0), axis=-1, keepdims=True)
        gates = jnp.where(lane == k, exps[k] / denom, gates)
        eids = jnp.where(lane == k, top_ids[k], eids)
        ranks = jnp.where(lane == k, rank_k.astype(I32), ranks)
    gates_ref[...] = gates
    eid_ref[...] = eids
    rank_ref[...] = ranks
    carry_ref[...] = carry_ref[...] + incl[tm - 1:tm, :]
    counts_ref[...] = carry_ref[...].astype(I32)


def _route(x, g, sc, sh, w_router_pad, b_router_pad, *, seq_len, n_experts):
    t, d = x.shape
    tm = _tile(seq_len, 512)
    per_seq = seq_len // tm
    tok_spec = pl.BlockSpec((tm, LANES), lambda i: (i, 0))
    return pl.pallas_call(
        functools.partial(_router_kernel, n_experts=n_experts),
        out_shape=[
            jax.ShapeDtypeStruct((t, d // 2), U32),
            jax.ShapeDtypeStruct((t, LANES), F32),
            jax.ShapeDtypeStruct((t, LANES), I32),
            jax.ShapeDtypeStruct((t, LANES), I32),
            jax.ShapeDtypeStruct((1, LANES), I32),
        ],
        grid=(t // tm,),
        in_specs=[
            pl.BlockSpec((tm, d), lambda i: (i, 0)),
            pl.BlockSpec((1, d), lambda i: (0, 0)),
            pl.BlockSpec((1, 1, d), lambda i: (i // per_seq, 0, 0)),
            pl.BlockSpec((1, 1, d), lambda i: (i // per_seq, 0, 0)),
            pl.BlockSpec((d, LANES), lambda i: (0, 0)),
            pl.BlockSpec((1, LANES), lambda i: (0, 0)),
        ],
        out_specs=[
            pl.BlockSpec((tm, d // 2), lambda i: (i, 0)),
            tok_spec, tok_spec, tok_spec,
            pl.BlockSpec((1, LANES), lambda i: (0, 0)),
        ],
        scratch_shapes=[pltpu.VMEM((1, LANES), F32)],
        compiler_params=_params("arbitrary"),
        name="moe_router",
    )(x, g, sc, sh, w_router_pad, b_router_pad)


def _dispatch_kernel(dest_ref, pad_lo_ref, pad_hi_ref, hp_ref, xs_ref,
                     stage_ref, zero_ref, stage_sems, row_sems, pad_sem):
    n_slots, tm, _ = stage_ref.shape
    step = pl.program_id(0)
    n_steps = pl.num_programs(0)
    slot = lax.rem(step, n_slots)
    parity = lax.rem(step, 2)

    def stage_copy(tile, into):
        rows = pl.ds(pl.multiple_of(tile * tm, tm), tm)
        return pltpu.make_async_copy(hp_ref.at[rows], stage_ref.at[into], stage_sems.at[into])

    @pl.when(step == 0)
    def _():
        stage_copy(0, 0).start()
        zero_ref[...] = jnp.zeros_like(zero_ref)
        chunk = zero_ref.shape[0]

        def chunk_copy(hi, i):
            start = pl.multiple_of(hi - (i + 1) * chunk, chunk)
            return pltpu.make_async_copy(zero_ref, xs_ref.at[pl.ds(start, chunk)], pad_sem)

        def row_copy(r):
            return pltpu.make_async_copy(zero_ref.at[pl.ds(0, 1)], xs_ref.at[pl.ds(r, 1)], pad_sem)

        def for_each_copy(act):
            for e in range(pad_lo_ref.shape[0]):
                lo, hi = pad_lo_ref[e], pad_hi_ref[e]
                n_chunks = (hi - lo) // chunk
                lax.fori_loop(0, n_chunks, lambda i, c: (act(chunk_copy(hi, i)), c)[1], 0)
                lax.fori_loop(lo, hi - n_chunks * chunk, lambda r, c: (act(row_copy(r)), c)[1], 0)

        for_each_copy(lambda cp: cp.start())
        for_each_copy(lambda cp: cp.wait())

    stage_copy(step, slot).wait()

    @pl.when(step + 1 < n_steps)
    def _():
        stage_copy(step + 1, lax.rem(step + 1, n_slots)).start()

    def issue(t, carry):
        for k in range(TOP_K):
            d = dest_ref[(step * tm + t) * TOP_K + k]
            pltpu.make_async_copy(stage_ref.at[slot, pl.ds(t, 1)], xs_ref.at[pl.ds(d, 1)],
                                  row_sems.at[parity]).start(priority=k % 2)
        return carry

    lax.fori_loop(0, tm, issue, 0, unroll=DMA_ISSUE_UNROLL)

    def wait_rows(which):
        for _ in range(TOP_K):
            pltpu.make_async_copy(stage_ref.at[0], xs_ref.at[pl.ds(0, tm)], row_sems.at[which]).wait()

    @pl.when(step > 0)
    def _():
        wait_rows(1 - parity)

    @pl.when(step == n_steps - 1)
    def _():
        wait_rows(parity)


def _dispatch(dest, pad_lo, pad_hi, hp, n_rows):
    t, half = hp.shape
    tm = _tile(t, 256)
    return pl.pallas_call(
        _dispatch_kernel,
        out_shape=jax.ShapeDtypeStruct((n_rows, half), U32),
        grid_spec=pltpu.PrefetchScalarGridSpec(
            num_scalar_prefetch=3,
            grid=(t // tm,),
            in_specs=[pl.BlockSpec(memory_space=pl.ANY)],
            out_specs=pl.BlockSpec(memory_space=pl.ANY),
            scratch_shapes=[pltpu.VMEM((DISPATCH_SLOTS, tm, half), U32),
                            pltpu.VMEM((ZERO_CHUNK_ROWS, half), U32),
                            pltpu.SemaphoreType.DMA((DISPATCH_SLOTS,)),
                            pltpu.SemaphoreType.DMA((2,)), pltpu.SemaphoreType.DMA(())],
        ),
        compiler_params=_params("arbitrary"),
        name="moe_dispatch",
    )(dest, pad_lo, pad_hi, hp)


def _expert_kernel(blk_expert_ref, next_expert_ref, n_used_ref, xs_ref, wu_hbm, bu_ref, wd_hbm, bd_ref, y_ref,
                   wu_f32, wd_f32, wu_bf, wd_bf, xb_ref, sems, *, layer):
    half = xs_ref.shape[1]
    f = wd_bf.shape[0]
    b = pl.program_id(0)
    live = b < n_used_ref[0]
    expert = blk_expert_ref[b]
    opens_expert = jnp.logical_or(b == 0, blk_expert_ref[jnp.maximum(b - 1, 0)] != expert)

    def weight_copies(e):
        return (pltpu.make_async_copy(wu_hbm.at[layer, e], wu_f32, sems.at[0]),
                pltpu.make_async_copy(wd_hbm.at[layer, e], wd_f32, sems.at[1]))

    @pl.when(jnp.logical_and(live, opens_expert))
    def _():
        @pl.when(b == 0)
        def _():
            for cp in weight_copies(expert):
                cp.start()

        for cp in weight_copies(expert):
            cp.wait()
        wu_bf[...] = wu_f32[...].astype(BF16)
        wd_bf[...] = wd_f32[...].astype(BF16)
        upcoming = next_expert_ref[b]

        @pl.when(upcoming >= 0)
        def _():
            for cp in weight_copies(upcoming):
                cp.start()

    @pl.when(live)
    def _():
        lo, hi = _unpack_halves(xs_ref[...])
        xb_ref[:, :half] = lo.astype(BF16)
        xb_ref[:, half:] = hi.astype(BF16)
        rows_all = xs_ref.shape[0]
        sub = rows_all // 2
        wide = 2 * MXU_COLS
        row_halves = [slice(s * sub, (s + 1) * sub) for s in range(2)]
        wide = min(wide, wu_bf.shape[1])
        hu_parts = [[jnp.dot(xb_ref[rows, :], wu_bf[:, :wide], preferred_element_type=F32)] for rows in row_halves]
        if wide < wu_bf.shape[1]:
            for s, rows in enumerate(row_halves):
                hu_parts[s].append(jnp.dot(xb_ref[rows, :], wu_bf[:, wide:], preferred_element_type=F32))
        for s, rows in enumerate(row_halves):
            hu = jnp.concatenate(hu_parts[s], axis=1) + bu_ref[0]
            glu = jnp.minimum(hu[:, :f], SWIGLU_LIMIT)
            lin = jnp.clip(hu[:, f:], -SWIGLU_LIMIT, SWIGLU_LIMIT)
            act = (glu * jax.nn.sigmoid(SWIGLU_ALPHA * glu) * (lin + 1.0)).astype(BF16)
            cw = _tile(half, 512)
            for c in range(half // cw):
                pair = []
                for cols in (slice(c * cw, (c + 1) * cw), slice(half + c * cw, half + (c + 1) * cw)):
                    y = jnp.dot(act, wd_bf[:, cols], preferred_element_type=F32) + bd_ref[0, :, cols]
                    pair.append(lax.bitcast_convert_type(y.astype(BF16).astype(F32), U32))
                y_ref[rows, c * cw:(c + 1) * cw] = (pair[0] >> 16) | (pair[1] & U32(HI_MASK))

    @pl.when(jnp.logical_not(live))
    def _():
        y_ref[...] = jnp.zeros_like(y_ref)


def _experts(blk_expert, next_expert, n_used, xs, w_up, b_up, w_down, b_down, *, layer, block_rows):
    n_rows, half = xs.shape
    _, n_experts, d, f2 = w_up.shape
    f = f2 // 2
    n_blocks = n_rows // block_rows

    def row_block(b, blk_expert, next_expert, n_used):
        return (jnp.minimum(b, n_used[0] - 1), 0)

    def expert_block(b, blk_expert, next_expert, n_used):
        return (blk_expert[b], 0, 0)

    return pl.pallas_call(
        functools.partial(_expert_kernel, layer=layer),
        out_shape=jax.ShapeDtypeStruct((n_rows, half), U32),
        grid_spec=pltpu.PrefetchScalarGridSpec(
            num_scalar_prefetch=3,
            grid=(n_blocks,),
            in_specs=[
                pl.BlockSpec((block_rows, half), row_block),
                pl.BlockSpec(memory_space=pl.ANY),
                pl.BlockSpec((1, 1, f2), expert_block),
                pl.BlockSpec(memory_space=pl.ANY),
                pl.BlockSpec((1, 1, d), expert_block),
            ],
            out_specs=pl.BlockSpec((block_rows, half), lambda b, *_: (b, 0)),
            scratch_shapes=[
                pltpu.VMEM((d, f2), F32), pltpu.VMEM((f, d), F32),
                pltpu.VMEM((d, f2), BF16), pltpu.VMEM((f, d), BF16),
                pltpu.VMEM((block_rows, d), BF16),
                pltpu.SemaphoreType.DMA((2,)),
            ],
        ),
        compiler_params=_params("arbitrary"),
        name="moe_experts",
    )(blk_expert, next_expert, n_used, xs, w_up, b_up.reshape(n_experts, 1, f2),
      w_down, b_down.reshape(n_experts, 1, d))


def _combine_kernel(dest_ref, yb_ref, x_ref, gates_ref, gt_ref, o_ref, buf_ref, sems):
    tm = x_ref.shape[0]
    half = buf_ref.shape[3]
    step = pl.program_id(0)
    slot = lax.rem(step, 2)

    n_groups = tm // SUBLANES
    half_groups = n_groups // 2
    half_rows = tm // 2

    def request_rows(tile, into, group, part):
        for j in range(SUBLANES):
            t = group * SUBLANES + j
            for k in range(TOP_K):
                d = dest_ref[(tile * tm + t) * TOP_K + k]
                pltpu.make_async_copy(yb_ref.at[pl.ds(d, 1)], buf_ref.at[into, k, pl.ds(t, 1)],
                                      sems.at[into, part]).start()

    def wait_rows(part):
        for k in range(TOP_K):
            pltpu.make_async_copy(yb_ref.at[pl.ds(0, half_rows)], buf_ref.at[slot, k, pl.ds(0, half_rows)],
                                  sems.at[slot, part]).wait()

    def weighted_sum(group):
        rows = pl.ds(pl.multiple_of(group * SUBLANES, SUBLANES), SUBLANES)
        acc_lo = acc_hi = None
        for k in range(TOP_K):
            lo, hi = _unpack_halves(buf_ref[slot, k, rows, :])
            gate = gates_ref[rows, k:k + 1]
            acc_lo = gate * lo if acc_lo is None else acc_lo + gate * lo
            acc_hi = gate * hi if acc_hi is None else acc_hi + gate * hi
        o_ref[rows, :half] = x_ref[rows, :half] + gt_ref[0, :, :half] * acc_lo
        o_ref[rows, half:] = x_ref[rows, half:] + gt_ref[0, :, half:] * acc_hi

    @pl.when(step == 0)
    def _():
        for part in range(2):
            lax.fori_loop(part * half_groups, (part + 1) * half_groups,
                          lambda g, c, part=part: (request_rows(0, 0, g, part), c)[1], 0)

    has_next = step + 1 < pl.num_programs(0)
    for part in range(2):
        wait_rows(part)
        first, last = part * half_groups, (part + 1) * half_groups

        @pl.when(has_next)
        def _():
            def body(g, c):
                request_rows(step + 1, 1 - slot, g, part)
                weighted_sum(g)
                return c

            lax.fori_loop(first, last, body, 0)

        @pl.when(jnp.logical_not(has_next))
        def _():
            lax.fori_loop(first, last, lambda g, c: (weighted_sum(g), c)[1], 0)


def _combine(dest, yb, x, gates, gt, *, seq_len):
    t, d = x.shape
    half = d // 2
    tm = _tile(seq_len, 256)
    per_seq = seq_len // tm
    return pl.pallas_call(
        _combine_kernel,
        out_shape=jax.ShapeDtypeStruct((t, d), F32),
        grid_spec=pltpu.PrefetchScalarGridSpec(
            num_scalar_prefetch=1,
            grid=(t // tm,),
            in_specs=[
                pl.BlockSpec(memory_space=pl.ANY),
                pl.BlockSpec((tm, d), lambda i, dest: (i, 0)),
                pl.BlockSpec((tm, LANES), lambda i, dest: (i, 0)),
                pl.BlockSpec((1, 1, d), lambda i, dest: (i // per_seq, 0, 0)),
            ],
            out_specs=pl.BlockSpec((tm, d), lambda i, dest: (i, 0)),
            scratch_shapes=[pltpu.VMEM((2, TOP_K, tm, half), U32), pltpu.SemaphoreType.DMA((2, 2))],
        ),
        input_output_aliases={2: 0},
        compiler_params=_params("arbitrary"),
        name="moe_combine",
    )(dest, yb, x, gates, gt)


def _moe_layer(x, g, sc, sh, gt, w_router, b_router, w_up, b_up, w_down, b_down, *, layer, seq_len, block_rows):
    t, d = x.shape
    n_experts = w_router.shape[1]
    assert 2 * n_experts <= LANES
    w_router_pad = jnp.pad(jnp.concatenate([w_router, w_router], axis=1), ((0, 0), (0, LANES - 2 * n_experts)))
    b_router_pad = jnp.pad(b_router, (0, LANES - n_experts)).reshape(1, LANES)
    hp, gates, eids, ranks, counts = _route(x, g, sc, sh, w_router_pad, b_router_pad,
                                            seq_len=seq_len, n_experts=n_experts)
    counts = counts[0, :n_experts]
    padded = (counts + block_rows - 1) // block_rows * block_rows
    seg_end = jnp.cumsum(padded)
    seg_start = seg_end - padded
    n_rows = t * TOP_K + n_experts * block_rows
    n_blocks = n_rows // block_rows
    expert_ids = jnp.arange(n_experts, dtype=I32)
    is_expert = eids[:, :TOP_K, None] == expert_ids
    dest = (ranks[:, :TOP_K] + jnp.sum(jnp.where(is_expert, seg_start, 0), axis=-1)).reshape(-1).astype(I32)
    blk_start = jnp.arange(n_blocks, dtype=I32) * block_rows
    blk_expert = jnp.minimum(jnp.sum(blk_start[:, None] >= seg_end[None, :], axis=1), n_experts - 1).astype(I32)
    n_used = (seg_end[-1:] // block_rows).astype(I32)
    later = lax.cummin(jnp.where(counts > 0, expert_ids, n_experts), axis=0, reverse=True)
    following = jnp.concatenate([later[1:], jnp.full((1,), n_experts, I32)])
    next_expert = jnp.where(following < n_experts, following, -1)[blk_expert].astype(I32)

    assert block_rows % ZERO_CHUNK_ROWS == 0
    pad_lo = jnp.concatenate([seg_start + counts, seg_end[-1:]]).astype(I32)
    pad_hi = jnp.concatenate([seg_end, jnp.full((1,), n_rows, I32)]).astype(I32)
    xs = _dispatch(dest, pad_lo, pad_hi, hp, n_rows)
    yb = _experts(blk_expert, next_expert, n_used, xs, w_up, b_up, w_down, b_down,
                  layer=layer, block_rows=block_rows)
    return _combine(dest, yb, x, gates, gt, seq_len=seq_len)


def _final_norm_kernel(x_ref, g_ref, o_ref):
    x = x_ref[...]
    o_ref[...] = x * lax.rsqrt(jnp.mean(x * x, axis=-1, keepdims=True) + NORM_EPS) * g_ref[...]


def _final_norm(x, g):
    t, d = x.shape
    tm = _tile(t, 512)
    return pl.pallas_call(
        _final_norm_kernel,
        out_shape=jax.ShapeDtypeStruct((t, d), F32),
        grid=(t // tm,),
        in_specs=[pl.BlockSpec((tm, d), lambda i: (i, 0)), pl.BlockSpec((1, d), lambda i: (0, 0))],
        out_specs=pl.BlockSpec((tm, d), lambda i: (i, 0)),
        compiler_params=_params("parallel"),
        name="final_norm",
    )(x, g)


def kernel(x, c, w_cond, b_cond, w_mod, b_mod, g_norm_mix, g_norm_ffn, pool_w_in, pool_w_group, pool_scale, pool_w_out, gmlp_w_in, gmlp_b_in, gmlp_ln_g, gmlp_ln_b, gmlp_w_s, gmlp_b_s, gmlp_w_out, moe_w_router, moe_b_router, moe_w_up, moe_b_up, moe_w_down, moe_b_down, g_final):
    bsz, seq_len, d = x.shape
    depth = w_mod.shape[0]
    mod = _modulation(c, w_cond, b_cond, w_mod, b_mod)
    xt = x.reshape(bsz * seq_len, d)
    zero_bias = jnp.zeros((1, d), F32)
    for layer in range(depth):
        sh1, sc1, gt1, sh2, sc2, gt2 = (mod[layer, :, m] for m in range(6))
        j = layer // 2
        g_mix = g_norm_mix[layer].reshape(1, d)
        if layer % 2 == 0:
            u = _norm_matmul(xt, g_mix, sc1, sh1, _bf16_weight(pool_w_in, j), zero_bias,
                             seq_len=seq_len, gelu=False)
            n_groups, gw, _ = pool_w_group.shape[1:]
            w_group = _bf16_weight(pool_w_group.reshape(-1, n_groups * gw, gw), j).reshape(n_groups, gw, gw)
            mid = _pool_middle(u, w_group, pool_scale[j].reshape(1, d), seq_len=seq_len)
            w_out = _bf16_weight(pool_w_out, j)
        else:
            uv = _norm_matmul(xt, g_mix, sc1, sh1, _bf16_weight(gmlp_w_in, j), gmlp_b_in[j].reshape(1, -1),
                              seq_len=seq_len, gelu=True)
            mid = _sgu_middle(uv, gmlp_ln_g[j].reshape(1, -1), gmlp_ln_b[j].reshape(1, -1),
                              gmlp_w_s[j], gmlp_b_s[j].T, seq_len=seq_len)
            w_out = _bf16_weight(gmlp_w_out, j)
        xt = _matmul_residual(mid, w_out, xt, gt1, seq_len=seq_len)
        xt = _moe_layer(xt, g_norm_ffn[layer].reshape(1, d), sc2, sh2, gt2,
                        moe_w_router[layer], moe_b_router[layer],
                        moe_w_up, moe_b_up[layer], moe_w_down, moe_b_down[layer],
                        layer=layer, seq_len=seq_len, block_rows=512)
    return _final_norm(xt, g_final.reshape(1, d)).reshape(bsz, seq_len, d)
```
